```python
import jax, jax.numpy as jnp
from jax import lax
import numpy as np

D_MODEL = 1024
BATCH = 2
SEQ = 8192
DEPTH = 4

N_MIXERS = 4
N_MEM = 256
MIX_WIDTH = D_MODEL
XATTN_HEADS = 4
XATTN_HEAD_DIM = 64
XATTN_WIDTH = XATTN_HEADS * XATTN_HEAD_DIM
TOK_WIDTH = MIX_WIDTH - XATTN_WIDTH

GMLP_CHUNK = 128
GMLP_HEAD_DIM = 128
GMLP_HEADS = TOK_WIDTH // GMLP_HEAD_DIM

HGRN_HEAD_DIM = 128
HGRN_HEADS = TOK_WIDTH // HGRN_HEAD_DIM
HGRN_CHUNK = 16

POOL_WINDOWS = (2, 4, 8, 16)
POOL_GROUP = TOK_WIDTH // len(POOL_WINDOWS)

LRU_HEAD_DIM = 128
LRU_HEADS = TOK_WIDTH // LRU_HEAD_DIM
CONV_WIDTH = 4
LRU_C = 8.0

ALPHA = (2 * DEPTH) ** 0.25
BETA = (8 * DEPTH) ** -0.25
LN_EPS = 1e-5
RMS_EPS = 1e-6

IN_WIDTH_A = 2 * TOK_WIDTH + XATTN_WIDTH + MIX_WIDTH
IN_WIDTH_B = 3 * TOK_WIDTH + XATTN_WIDTH + MIX_WIDTH
IN_WIDTH_C = TOK_WIDTH + XATTN_WIDTH + MIX_WIDTH
IN_WIDTH_D = TOK_WIDTH + XATTN_WIDTH + MIX_WIDTH

kernel_name = 'hybrid_gmlp_hgrn2_pool_rglru_trunk'


def n_of_kind(kind):
    return len(range(kind, DEPTH, N_MIXERS))


def split_cols(t, widths):
    idx = [int(v) for v in np.cumsum(widths)[:-1]]
    return jnp.split(t, idx, axis=-1)


def layer_norm(x, g, b):
    xf = x.astype(jnp.float32)
    mu = jnp.mean(xf, -1, keepdims=True)
    var = jnp.mean(jnp.square(xf - mu), -1, keepdims=True)
    y = (xf - mu) * lax.rsqrt(var + LN_EPS) * g.astype(jnp.float32) + b.astype(jnp.float32)
    return y.astype(x.dtype)


def memory_cross_attention(q, mem_k, mem_v):
    B, S, _ = q.shape
    qh = q.reshape(B, S, XATTN_HEADS, XATTN_HEAD_DIM)
    s = jnp.einsum('bshd,bmhd->bhsm', qh, mem_k).astype(jnp.float32) * (XATTN_HEAD_DIM ** -0.5)
    p = jax.nn.softmax(s, axis=-1).astype(mem_v.dtype)
    o = jnp.einsum('bhsm,bmhd->bshd', p, mem_v)
    return o.reshape(B, S, XATTN_WIDTH).astype(q.dtype)


def chunked_spatial_gating(u, v, w_s, b_s):
    dt = u.dtype
    B, S, _ = v.shape
    n = S // GMLP_CHUNK
    u = jax.nn.gelu(u.astype(jnp.float32))
    v = jax.nn.gelu(v.astype(jnp.float32))
    vg = v.reshape(B, n, GMLP_CHUNK, GMLP_HEADS, GMLP_HEAD_DIM)
    mu = jnp.mean(vg, -1, keepdims=True)
    var = jnp.mean(jnp.square(vg - mu), -1, keepdims=True)
    vn = (vg - mu) * lax.rsqrt(var + LN_EPS)
    causal = jnp.tril(jnp.ones((GMLP_CHUNK, GMLP_CHUNK), dtype=bool))
    w = jnp.where(causal[None], w_s.astype(jnp.float32), 0.0)
    mixed = jnp.einsum('gts,bnsgc->bntgc', w, vn) + b_s.astype(jnp.float32).T[:, :, None]
    return (u * mixed.reshape(B, S, TOK_WIDTH)).astype(dt)


def hgrn2(q, f_logit, i, lb, norm_g):
    dt = q.dtype
    B, S, _ = q.shape
    H, K, C = HGRN_HEADS, HGRN_HEAD_DIM, HGRN_CHUNK
    n = S // C
    f = lb + (1.0 - lb) * jax.nn.sigmoid(f_logit.astype(jnp.float32))
    log_f = jnp.log(f)
    k = 1.0 - f
    qf = jax.nn.silu(q.astype(jnp.float32))
    vf = i.astype(jnp.float32)

    def chunks(t):
        return t.reshape(B, n, C, H, K).transpose(1, 0, 2, 3, 4)

    causal = jnp.tril(jnp.ones((C, C), dtype=bool))

    def step(state, xs):
        qc, kc, vc, lfc = xs
        g = jnp.cumsum(lfc, axis=1)
        g_last = g[:, -1]
        q_dec = qc * jnp.exp(g)
        k_inv = kc * jnp.exp(-g)
        scores = jnp.einsum('bthk,bshk->bhts', q_dec, k_inv)
        scores = jnp.where(causal, scores, 0.0)
        o = (jnp.einsum('bhts,bshv->bthv', scores, vc)
             + jnp.einsum('bthk,bhkv->bthv', q_dec, state))
        k_end = kc * jnp.exp(g_last[:, None] - g)
        state = jnp.exp(g_last)[..., None] * state + jnp.einsum('bshk,bshv->bhkv', k_end, vc)
        return state, o

    s0 = jnp.zeros((B, H, K, K), jnp.float32)
    _, o = lax.scan(step, s0, (chunks(qf), chunks(k), chunks(vf), chunks(log_f)))
    o = o.transpose(1, 0, 2, 3, 4).reshape(B, S, H, K)
    o = o * lax.rsqrt(jnp.mean(jnp.square(o), -1, keepdims=True) + RMS_EPS)
    return (o.reshape(B, S, TOK_WIDTH) * norm_g.astype(jnp.float32)).astype(dt)


def multiscale_pool(p, w_pool, scale):
    dt = p.dtype
    B, S, _ = p.shape
    grp = p.astype(jnp.float32).reshape(B, S, len(POOL_WINDOWS), POOL_GROUP)
    cs = jnp.cumsum(grp, axis=1)
    pos = jnp.arange(S)
    pooled = []
    for g, w in enumerate(POOL_WINDOWS):
        csg = cs[:, :, g]
        prev = jnp.pad(csg, ((0, 0), (w, 0), (0, 0)))[:, :S]
        cnt = jnp.minimum(pos + 1, w).astype(jnp.float32)[None, :, None]
        pooled.append((csg - prev) / cnt)
    pooled = jnp.stack(pooled, axis=2)
    y = jnp.einsum('bsgc,gcd->bsgd', pooled - grp, w_pool.astype(jnp.float32))
    return (y.reshape(B, S, TOK_WIDTH) * scale.astype(jnp.float32)).astype(dt)


def rg_lru_branch(xb, conv_w, conv_b, w_gx, b_gx, w_ga, b_ga, a_param):
    dt = xb.dtype
    B, S, _ = xb.shape
    xc = lax.conv_general_dilated(
        xb, conv_w.astype(dt)[:, None, :], window_strides=(1,),
        padding=[(CONV_WIDTH - 1, 0)], dimension_numbers=('NWC', 'WIO', 'NWC'),
        feature_group_count=TOK_WIDTH) + conv_b.astype(dt)
    xh = xc.astype(jnp.float32).reshape(B, S, LRU_HEADS, LRU_HEAD_DIM)
    gate_x = jax.nn.sigmoid(jnp.einsum('bshi,hij->bshj', xh, w_gx.astype(jnp.float32)) + b_gx.astype(jnp.float32))
    gate_a = jax.nn.sigmoid(jnp.einsum('bshi,hij->bshj', xh, w_ga.astype(jnp.float32)) + b_ga.astype(jnp.float32))
    log_a = -LRU_C * gate_a * jax.nn.softplus(-a_param.astype(jnp.float32).reshape(LRU_HEADS, LRU_HEAD_DIM))
    a = jnp.exp(log_a)
    mult = jnp.sqrt(-jnp.expm1(2.0 * log_a))
    first = (jnp.arange(S) == 0)[None, :, None, None]
    mult = jnp.where(first, 1.0, mult)
    b_term = mult * gate_x * xh

    def combine(l, r):
        a1, b1 = l
        a2, b2 = r
        return a1 * a2, a2 * b1 + b2

    _, h = lax.associative_scan(combine, (a, b_term), axis=1)
    return h.reshape(B, S, TOK_WIDTH).astype(dt)


def setup_inputs(seed: int = 0) -> dict:
    key = jax.random.key(seed)
    ks = jax.random.split(key, 24)
    f32 = jnp.float32

    def nrm(k, shape, scale):
        return jax.random.normal(k, shape, f32) * scale

    nA, nB, nC, nD = n_of_kind(0), n_of_kind(1), n_of_kind(2), n_of_kind(3)
    u = jax.random.uniform(ks[22], (nD, TOK_WIDTH), f32, minval=0.9, maxval=0.999)
    s = u ** (1.0 / LRU_C)
    return {
        'x': nrm(ks[0], (BATCH, SEQ, D_MODEL), 1.0),
        'mem': nrm(ks[1], (BATCH, N_MEM, D_MODEL), 1.0),
        'mem_kv_w': nrm(ks[2], (D_MODEL, 2 * XATTN_WIDTH), D_MODEL ** -0.5),
        'ln_g': 1.0 + nrm(ks[3], (DEPTH, D_MODEL), 0.02),
        'ln_b': nrm(ks[4], (DEPTH, D_MODEL), 0.02),
        'w_out': nrm(ks[5], (DEPTH, MIX_WIDTH, D_MODEL), BETA * MIX_WIDTH ** -0.5),
        'hgrn_lb_logits': 1.0 + nrm(ks[6], (DEPTH, TOK_WIDTH), 0.1),
        'a_w_in': nrm(ks[7], (nA, D_MODEL, IN_WIDTH_A), D_MODEL ** -0.5),
        'a_w_s': nrm(ks[8], (nA, GMLP_HEADS, GMLP_CHUNK, GMLP_CHUNK), GMLP_CHUNK ** -0.5),
        'a_b_s': 1.0 + nrm(ks[9], (nA, GMLP_HEADS, GMLP_CHUNK), 0.02),
        'b_w_in': nrm(ks[10], (nB, D_MODEL, IN_WIDTH_B), D_MODEL ** -0.5),
        'b_norm_g': 1.0 + nrm(ks[11], (nB, TOK_WIDTH), 0.02),
        'c_w_in': nrm(ks[12], (nC, D_MODEL, IN_WIDTH_C), D_MODEL ** -0.5),
        'c_w_pool': nrm(ks[13], (nC, len(POOL_WINDOWS), POOL_GROUP, POOL_GROUP), POOL_GROUP ** -0.5),
        'c_scale': 1.0 + nrm(ks[14], (nC, TOK_WIDTH), 0.02),
        'd_w_in': nrm(ks[15], (nD, D_MODEL, IN_WIDTH_D), D_MODEL ** -0.5),
        'd_conv_w': nrm(ks[16], (nD, CONV_WIDTH, TOK_WIDTH), CONV_WIDTH ** -0.5),
        'd_conv_b': nrm(ks[17], (nD, TOK_WIDTH), 0.02),
        'd_w_gx': nrm(ks[18], (nD, LRU_HEADS, LRU_HEAD_DIM, LRU_HEAD_DIM), LRU_HEAD_DIM ** -0.5),
        'd_b_gx': nrm(ks[19], (nD, LRU_HEADS, LRU_HEAD_DIM), 0.02),
        'd_w_ga': nrm(ks[20], (nD, LRU_HEADS, LRU_HEAD_DIM, LRU_HEAD_DIM), LRU_HEAD_DIM ** -0.5),
        'd_b_ga': nrm(ks[21], (nD, LRU_HEADS, LRU_HEAD_DIM), 0.02),
        'd_a_param': jnp.log(s) - jnp.log1p(-s),
    }


def reference(x, mem, mem_kv_w, ln_g, ln_b, w_out, hgrn_lb_logits,
              a_w_in, a_w_s, a_b_s,
              b_w_in, b_norm_g,
              c_w_in, c_w_pool, c_scale,
              d_w_in, d_conv_w, d_conv_b, d_w_gx, d_b_gx, d_w_ga, d_b_ga, d_a_param):
    B = x.shape[0]
    M = mem.shape[1]
    kv = jnp.einsum('bmd,de->bme', mem, mem_kv_w)
    mem_k = kv[..., :XATTN_WIDTH].reshape(B, M, XATTN_HEADS, XATTN_HEAD_DIM)
    mem_v = kv[..., XATTN_WIDTH:].reshape(B, M, XATTN_HEADS, XATTN_HEAD_DIM)
    lb_p = jax.nn.softmax(hgrn_lb_logits.astype(jnp.float32), axis=0)
    lower_bounds = jnp.cumsum(lb_p, axis=0) - lb_p[0]

    for i in range(DEPTH):
        kind, j = i % N_MIXERS, i // N_MIXERS
        if kind == 0:
            proj = jnp.einsum('bsd,de->bse', x, a_w_in[j])
            u, v, q_x, gate = split_cols(proj, [TOK_WIDTH, TOK_WIDTH, XATTN_WIDTH, MIX_WIDTH])
            tok = chunked_spatial_gating(u, v, a_w_s[j], a_b_s[j])
        elif kind == 1:
            proj = jnp.einsum('bsd,de->bse', x, b_w_in[j])
            q, f_logit, inp, q_x, gate = split_cols(
                proj, [TOK_WIDTH, TOK_WIDTH, TOK_WIDTH, XATTN_WIDTH, MIX_WIDTH])
            tok = hgrn2(q, f_logit, inp, lower_bounds[i], b_norm_g[j])
        elif kind == 2:
            proj = jnp.einsum('bsd,de->bse', x, c_w_in[j])
            p, q_x, gate = split_cols(proj, [TOK_WIDTH, XATTN_WIDTH, MIX_WIDTH])
            tok = multiscale_pool(p, c_w_pool[j], c_scale[j])
        else:
            proj = jnp.einsum('bsd,de->bse', x, d_w_in[j])
            xb, q_x, gate = split_cols(proj, [TOK_WIDTH, XATTN_WIDTH, MIX_WIDTH])
            tok = rg_lru_branch(xb, d_conv_w[j], d_conv_b[j], d_w_gx[j], d_b_gx[j],
                                d_w_ga[j], d_b_ga[j], d_a_param[j])
        xo = memory_cross_attention(q_x, mem_k, mem_v)
        mixed = jnp.concatenate([tok.astype(x.dtype), xo.astype(x.dtype)], axis=-1) * jax.nn.silu(gate)
        y = jnp.einsum('bse,ed->bsd', mixed, w_out[i]).astype(x.dtype)
        x = layer_norm(ALPHA * x + y, ln_g[i], ln_b[i])
    return x
```

```python
import functools

import jax
import jax.numpy as jnp
from jax import lax
from jax.experimental import pallas as pl
from jax.experimental.pallas import tpu as pltpu

F32 = jnp.float32
BF16 = jnp.bfloat16

D_MODEL = 1024
DEPTH = 4
N_MEM = 256
MIX_WIDTH = D_MODEL
XATTN_HEADS = 4
XATTN_HEAD_DIM = 64
XATTN_WIDTH = XATTN_HEADS * XATTN_HEAD_DIM
TOK_WIDTH = MIX_WIDTH - XATTN_WIDTH
HEAD_DIM = 128
N_HEADS = TOK_WIDTH // HEAD_DIM
CHUNK = 128
POOL_WINDOWS = (2, 4, 8, 16)
POOL_GROUP = TOK_WIDTH // len(POOL_WINDOWS)
POOL_HALO = 16
CONV_WIDTH = 4
CONV_HALO = 8
LRU_C = 8.0
ALPHA = (2 * DEPTH) ** 0.25
LN_EPS = 1e-5
RMS_EPS = 1e-6

SEQ_TILE = 256
VMEM_LIMIT_BYTES = 56 * 1024 * 1024


def _dot(a, b):
    return jnp.dot(a.astype(BF16), b.astype(BF16), preferred_element_type=F32)


def _dot_nt(a, b):
    return lax.dot_general(a.astype(BF16), b.astype(BF16), (((1,), (1,)), ((), ())),
                           preferred_element_type=F32)


def _dot_tn(a, b):
    return lax.dot_general(a.astype(BF16), b.astype(BF16), (((0,), (0,)), ((), ())),
                           preferred_element_type=F32)


def _silu(t):
    return t * jax.nn.sigmoid(t)


def _kv_kernel(mem_ref, w_ref, kbd_ref, vo_ref):
    kv = jnp.dot(mem_ref[0].astype(BF16), w_ref[...], preferred_element_type=F32)
    kt = kv[:, :XATTN_WIDTH].T
    v = kv[:, XATTN_WIDTH:]
    row_head = lax.broadcasted_iota(jnp.int32, (XATTN_WIDTH, N_MEM), 0) // XATTN_HEAD_DIM
    lane_head = lax.broadcasted_iota(jnp.int32, (N_MEM, XATTN_WIDTH), 1) // XATTN_HEAD_DIM
    for h in range(XATTN_HEADS):
        kbd_ref[0, :, h * N_MEM:(h + 1) * N_MEM] = jnp.where(row_head == h, kt, 0.0).astype(BF16)
        vo_ref[0, h * N_MEM:(h + 1) * N_MEM, :XATTN_WIDTH] = (
            jnp.where(lane_head == h, v, 0.0).astype(BF16))
        vo_ref[0, h * N_MEM:(h + 1) * N_MEM, XATTN_WIDTH:] = (
            jnp.where(lane_head == h, 1.0, 0.0).astype(BF16))


def _memory_operands(mem, mem_kv_w):
    batch = mem.shape[0]
    return pl.pallas_call(
        _kv_kernel,
        grid=(batch,),
        in_specs=[pl.BlockSpec((1, N_MEM, D_MODEL), lambda b: (b, 0, 0)),
                  pl.BlockSpec((D_MODEL, 2 * XATTN_WIDTH), lambda b: (0, 0))],
        out_specs=[pl.BlockSpec((1, XATTN_WIDTH, XATTN_HEADS * N_MEM), lambda b: (b, 0, 0)),
                   pl.BlockSpec((1, XATTN_HEADS * N_MEM, 2 * XATTN_WIDTH), lambda b: (b, 0, 0))],
        out_shape=[jax.ShapeDtypeStruct((batch, XATTN_WIDTH, XATTN_HEADS * N_MEM), BF16),
                   jax.ShapeDtypeStruct((batch, XATTN_HEADS * N_MEM, 2 * XATTN_WIDTH), BF16)],
        name="memory_kv",
    )(mem, mem_kv_w.astype(BF16))


def _cross_attention(qx, kbd_ref, vo_ref):
    s = jnp.dot(qx.astype(BF16), kbd_ref[0], preferred_element_type=F32) * (XATTN_HEAD_DIM ** -0.5)
    probs = []
    for h in range(XATTN_HEADS):
        sh = s[:, h * N_MEM:(h + 1) * N_MEM]
        probs.append(jnp.exp(sh - jnp.max(sh, axis=-1, keepdims=True)).astype(BF16))
    nd = jnp.dot(jnp.concatenate(probs, axis=-1), vo_ref[0], preferred_element_type=F32)
    return nd[:, :XATTN_WIDTH] / nd[:, XATTN_WIDTH:]


def _layer_tail(x, tok, qx, gate, kbd_ref, vo_ref, wout_ref, lng_ref, lnb_ref, o_ref):
    xo = _cross_attention(qx, kbd_ref, vo_ref)
    mixed = jnp.concatenate([tok, xo], axis=-1) * _silu(gate)
    y = jnp.dot(mixed.astype(BF16), wout_ref[...], preferred_element_type=F32)
    z = ALPHA * x + y
    zc = z - jnp.mean(z, axis=-1, keepdims=True)
    var = jnp.mean(zc * zc, axis=-1, keepdims=True)
    o_ref[0] = zc * lax.rsqrt(var + LN_EPS) * lng_ref[...] + lnb_ref[...]


def _project(xb, win_ref, start, width):
    return jnp.dot(xb, win_ref[:, start:start + width], preferred_element_type=F32)


def _gmlp_layer_kernel(x_ref, win_ref, ws_ref, bs_ref, kbd_ref, vo_ref, wout_ref, lng_ref, lnb_ref,
                       o_ref, tok_ref):
    x = x_ref[0]
    xb = x.astype(BF16)
    u = jax.nn.gelu(_project(xb, win_ref, 0, TOK_WIDTH))
    v = jax.nn.gelu(_project(xb, win_ref, TOK_WIDTH, TOK_WIDTH))
    qx = _project(xb, win_ref, 2 * TOK_WIDTH, XATTN_WIDTH)
    gate = _project(xb, win_ref, 2 * TOK_WIDTH + XATTN_WIDTH, MIX_WIDTH)
    rows = lax.broadcasted_iota(jnp.int32, (CHUNK, CHUNK), 0)
    cols = lax.broadcasted_iota(jnp.int32, (CHUNK, CHUNK), 1)
    causal = cols <= rows
    for g in range(N_HEADS):
        lanes = slice(g * HEAD_DIM, (g + 1) * HEAD_DIM)
        vg = v[:, lanes]
        vc = vg - jnp.mean(vg, axis=-1, keepdims=True)
        vn = (vc * lax.rsqrt(jnp.mean(vc * vc, axis=-1, keepdims=True) + LN_EPS)).astype(BF16)
        w = jnp.where(causal, ws_ref[g], 0.0).astype(BF16)
        bias = bs_ref[:, g:g + 1]
        for c in range(x.shape[0] // CHUNK):
            r = slice(c * CHUNK, (c + 1) * CHUNK)
            spatial = jnp.dot(w, vn[r], preferred_element_type=F32) + bias
            tok_ref[r, lanes] = u[r, lanes] * spatial
    _layer_tail(x, tok_ref[...], qx, gate, kbd_ref, vo_ref, wout_ref, lng_ref, lnb_ref, o_ref)


def _split3(t):
    hi = t.astype(BF16)
    r1 = t - hi.astype(F32)
    mid = r1.astype(BF16)
    lo = (r1 - mid.astype(F32)).astype(BF16)
    return hi, mid, lo


def _rows_at(g, row, picks, first):
    out = first
    for start, src in picks:
        out = jnp.where(row >= start, g[src:src + 1], out)
    return out


def _hgrn_layer_kernel(layer, x_ref, win_ref, lbl_ref, ng_ref, kbd_ref, vo_ref, wout_ref, lng_ref,
                       lnb_ref, o_ref, tok_ref, state_ref):
    @pl.when(pl.program_id(1) == 0)
    def _():
        state_ref[...] = jnp.zeros_like(state_ref)

    x = x_ref[0]
    xb = x.astype(BF16)
    q = _silu(_project(xb, win_ref, 0, TOK_WIDTH))
    f_logit = _project(xb, win_ref, TOK_WIDTH, TOK_WIDTH)
    val = _project(xb, win_ref, 2 * TOK_WIDTH, TOK_WIDTH)
    qx = _project(xb, win_ref, 3 * TOK_WIDTH, XATTN_WIDTH)
    gate = _project(xb, win_ref, 3 * TOK_WIDTH + XATTN_WIDTH, MIX_WIDTH)

    lbl = lbl_ref[...]
    lbe = jnp.exp(lbl - jnp.max(lbl, axis=0, keepdims=True))
    lbp = lbe / jnp.sum(lbe, axis=0, keepdims=True)
    lb = jnp.sum(lbp[:layer + 1], axis=0, keepdims=True) - lbp[0:1]
    f = lb + (1.0 - lb) * jax.nn.sigmoid(f_logit)
    log_f = jnp.log(f)
    k = 1.0 - f

    row = lax.broadcasted_iota(jnp.int32, (CHUNK, 1), 0)
    rr = lax.broadcasted_iota(jnp.int32, (CHUNK, CHUNK), 0)
    cc = lax.broadcasted_iota(jnp.int32, (CHUNK, CHUNK), 1)
    tril_ones = jnp.where(cc <= rr, 1.0, 0.0).astype(BF16)
    sub = 16
    levels = []
    half = CHUNK // 2
    while half >= sub:
        upper = (row % (2 * half)) >= half
        same = (rr // (2 * half)) == (cc // (2 * half))
        levels.append((half, upper, same))
        half //= 2
    diag = ((rr // sub) == (cc // sub)) & (cc <= rr)

    for c in range(x.shape[0] // CHUNK):
        r = slice(c * CHUNK, (c + 1) * CHUNK)
        for h in range(N_HEADS):
            lanes = slice(h * HEAD_DIM, (h + 1) * HEAD_DIM)
            qc, kc, vc = q[r, lanes], k[r, lanes], val[r, lanes]
            hi, mid, lo = _split3(log_f[r, lanes])
            g = (jnp.dot(tril_ones, hi, preferred_element_type=F32)
                 + jnp.dot(tril_ones, mid, preferred_element_type=F32)
                 + jnp.dot(tril_ones, lo, preferred_element_type=F32))
            g_last = g[CHUNK - 1:CHUNK]
            state_t = state_ref[h]
            o = _dot_nt(qc * jnp.exp(g), state_t)
            state_ref[h] = state_t * jnp.exp(g_last) + _dot_tn(vc, kc * jnp.exp(g_last - g))
            scores = None
            for half, upper, same in levels:
                picks = [(b, b + half - 1) for b in range(2 * half, CHUNK, 2 * half)]
                g_ref = _rows_at(g, row, picks, g[half - 1:half])
                qs = jnp.where(upper, qc * jnp.exp(g - g_ref), 0.0)
                ks = jnp.where(upper, 0.0, kc * jnp.exp(g_ref - g))
                a = _dot_nt(qs, ks)
                scores = a if scores is None else jnp.where(same, a, scores)
            picks = [(b, b - 1) for b in range(sub, CHUNK, sub)]
            g_ref = _rows_at(g, row, picks, jnp.zeros_like(g_last))
            a = _dot_nt(qc * jnp.exp(g - g_ref), kc * jnp.exp(g_ref - g))
            scores = jnp.where(diag, a, scores)
            o = o + _dot(scores, vc)
            o = o * lax.rsqrt(jnp.mean(o * o, axis=-1, keepdims=True) + RMS_EPS)
            tok_ref[r, lanes] = o * ng_ref[:, lanes]
    _layer_tail(x, tok_ref[...], qx, gate, kbd_ref, vo_ref, wout_ref, lng_ref, lnb_ref, o_ref)


def _pool_layer_kernel(x_ref, win_ref, wpool_ref, scale_ref, kbd_ref, vo_ref, wout_ref, lng_ref,
                       lnb_ref, o_ref, pbuf_ref):
    si = pl.program_id(1)
    tile = x_ref.shape[1]

    @pl.when(si == 0)
    def _():
        pbuf_ref[:POOL_HALO] = jnp.zeros((POOL_HALO, TOK_WIDTH), F32)

    x = x_ref[0]
    xb = x.astype(BF16)
    p = _project(xb, win_ref, 0, TOK_WIDTH)
    qx = _project(xb, win_ref, TOK_WIDTH, XATTN_WIDTH)
    gate = _project(xb, win_ref, TOK_WIDTH + XATTN_WIDTH, MIX_WIDTH)

    pbuf_ref[POOL_HALO:] = p
    ext = pbuf_ref[...]
    sums = []
    acc = ext
    for w in POOL_WINDOWS:
        acc = acc + pltpu.roll(acc, w // 2, 0)
        sums.append(acc[POOL_HALO:])
    pbuf_ref[:POOL_HALO] = p[tile - POOL_HALO:]

    pos = si * tile + lax.broadcasted_iota(jnp.int32, (tile, 1), 0)
    chan = lax.broadcasted_iota(jnp.int32, (1, TOK_WIDTH), 1)
    pooled = None
    for gi in reversed(range(len(POOL_WINDOWS))):
        w = POOL_WINDOWS[gi]
        mean = sums[gi] * (1.0 / jnp.minimum(pos + 1, w).astype(F32))
        pooled = mean if pooled is None else jnp.where(chan < (gi + 1) * POOL_GROUP, mean, pooled)
    tok = _dot(pooled - p, wpool_ref[...]) * scale_ref[...]
    _layer_tail(x, tok, qx, gate, kbd_ref, vo_ref, wout_ref, lng_ref, lnb_ref, o_ref)


def _lru_layer_kernel(x_ref, win_ref, convw_ref, convb_ref, wg_ref, bg_ref, ap_ref, kbd_ref, vo_ref,
                      wout_ref, lng_ref, lnb_ref, o_ref, cbuf_ref, a_ref, b_ref, carry_ref):
    si = pl.program_id(1)
    tile = x_ref.shape[1]

    @pl.when(si == 0)
    def _():
        cbuf_ref[:CONV_HALO] = jnp.zeros((CONV_HALO, TOK_WIDTH), F32)
        carry_ref[...] = jnp.zeros_like(carry_ref)

    x = x_ref[0]
    xb = x.astype(BF16)
    xin = _project(xb, win_ref, 0, TOK_WIDTH)
    qx = _project(xb, win_ref, TOK_WIDTH, XATTN_WIDTH)
    gate = _project(xb, win_ref, TOK_WIDTH + XATTN_WIDTH, MIX_WIDTH)

    cbuf_ref[CONV_HALO:] = xin
    ext = cbuf_ref[...]
    conv = convw_ref[CONV_WIDTH - 1:CONV_WIDTH] * ext
    for j in range(1, CONV_WIDTH):
        conv = conv + convw_ref[CONV_WIDTH - 1 - j:CONV_WIDTH - j] * pltpu.roll(ext, j, 0)
    xc = conv[CONV_HALO:] + convb_ref[...]
    cbuf_ref[:CONV_HALO] = xin[tile - CONV_HALO:]

    ap = ap_ref[...]
    neg_softplus = -(jnp.maximum(-ap, 0.0) + jnp.log1p(jnp.exp(-jnp.abs(ap))))
    first = (si * tile + lax.broadcasted_iota(jnp.int32, (tile, 1), 0)) == 0
    for h in range(N_HEADS):
        lanes = slice(h * HEAD_DIM, (h + 1) * HEAD_DIM)
        xh = xc[:, lanes]
        gates = jax.nn.sigmoid(_dot(xh, wg_ref[h]) + bg_ref[h])
        log_a = LRU_C * gates[:, HEAD_DIM:] * neg_softplus[:, lanes]
        th = jnp.tanh(-log_a)
        mult = jnp.where(first, 1.0, jnp.sqrt(2.0 * th / (1.0 + th)))
        a_ref[:, lanes] = jnp.exp(log_a)
        b_ref[:, lanes] = mult * gates[:, :HEAD_DIM] * xh

    a = a_ref[...]
    b = b_ref[...]
    row = lax.broadcasted_iota(jnp.int32, (tile, 1), 0)
    d = 1
    while d < tile:
        keep = row >= d
        b = a * jnp.where(keep, pltpu.roll(b, d, 0), 0.0) + b
        a = a * jnp.where(keep, pltpu.roll(a, d, 0), 1.0)
        d *= 2
    hseq = b + a * carry_ref[0:1]
    carry_ref[0:1] = hseq[tile - 1:tile]
    _layer_tail(x, hseq, qx, gate, kbd_ref, vo_ref, wout_ref, lng_ref, lnb_ref, o_ref)


def _full(arr):
    nd = arr.ndim
    return pl.BlockSpec(arr.shape, lambda b, s: (0,) * nd)


def _run_layer(body, x, params, kbd, vo, w_out, ln_g, ln_b, scratch, name):
    batch, seq, _ = x.shape
    tile = SEQ_TILE
    x_spec = pl.BlockSpec((1, tile, D_MODEL), lambda b, s: (b, s, 0))
    shared = [w_out.astype(BF16), ln_g.reshape(1, D_MODEL), ln_b.reshape(1, D_MODEL)]
    in_specs = ([x_spec] + [_full(p) for p in params]
                + [pl.BlockSpec((1,) + kbd.shape[1:], lambda b, s: (b, 0, 0)),
                   pl.BlockSpec((1,) + vo.shape[1:], lambda b, s: (b, 0, 0))]
                + [_full(p) for p in shared])
    return pl.pallas_call(
        body,
        grid=(batch, seq // tile),
        in_specs=in_specs,
        out_specs=x_spec,
        out_shape=jax.ShapeDtypeStruct(x.shape, x.dtype),
        scratch_shapes=scratch,
        compiler_params=pltpu.CompilerParams(
            dimension_semantics=("arbitrary", "arbitrary"), vmem_limit_bytes=VMEM_LIMIT_BYTES),
        name=name,
    )(x, *params, kbd, vo, *shared)


def kernel(x, mem, mem_kv_w, ln_g, ln_b, w_out, hgrn_lb_logits, a_w_in, a_w_s, a_b_s, b_w_in, b_norm_g,
           c_w_in, c_w_pool, c_scale, d_w_in, d_conv_w, d_conv_b, d_w_gx, d_b_gx, d_w_ga, d_b_ga,
           d_a_param):
    tile = SEQ_TILE
    assert x.shape[1] % tile == 0 and tile % CHUNK == 0
    kbd, vo = _memory_operands(mem, mem_kv_w)
    tok_scratch = pltpu.VMEM((tile, TOK_WIDTH), F32)
    for i in range(DEPTH):
        kind, j = i % 4, i // 4
        if kind == 0:
            params = [a_w_in[j].astype(BF16), a_w_s[j], a_b_s[j].T]
            body, scratch = _gmlp_layer_kernel, [tok_scratch]
        elif kind == 1:
            params = [b_w_in[j].astype(BF16), hgrn_lb_logits, b_norm_g[j].reshape(1, TOK_WIDTH)]
            body = functools.partial(_hgrn_layer_kernel, i)
            scratch = [tok_scratch, pltpu.VMEM((N_HEADS, HEAD_DIM, HEAD_DIM), F32)]
        elif kind == 2:
            wpool = jax.scipy.linalg.block_diag(*[c_w_pool[j, g] for g in range(len(POOL_WINDOWS))])
            params = [c_w_in[j].astype(BF16), wpool.astype(BF16), c_scale[j].reshape(1, TOK_WIDTH)]
            body = _pool_layer_kernel
            scratch = [pltpu.VMEM((POOL_HALO + tile, TOK_WIDTH), F32)]
        else:
            wg = jnp.concatenate([d_w_gx[j], d_w_ga[j]], axis=-1).astype(BF16)
            bg = jnp.concatenate([d_b_gx[j], d_b_ga[j]], axis=-1)[:, None, :]
            params = [d_w_in[j].astype(BF16), d_conv_w[j], d_conv_b[j].reshape(1, TOK_WIDTH), wg, bg,
                      d_a_param[j].reshape(1, TOK_WIDTH)]
            body = _lru_layer_kernel
            scratch = [pltpu.VMEM((CONV_HALO + tile, TOK_WIDTH), F32),
                       pltpu.VMEM((tile, TOK_WIDTH), F32), pltpu.VMEM((tile, TOK_WIDTH), F32),
                       pltpu.VMEM((8, TOK_WIDTH), F32)]
        x = _run_layer(body, x, params, kbd, vo, w_out[i], ln_g[i], ln_b[i], scratch,
                       name=f"layer{i}")
    return x
```

```python
import functools

import jax
import jax.numpy as jnp
from jax import lax
from jax.experimental import pallas as pl
from jax.experimental.pallas import tpu as pltpu

F32 = jnp.float32
BF16 = jnp.bfloat16

D_MODEL = 1024
DEPTH = 4
N_MEM = 256
MIX_WIDTH = D_MODEL
XATTN_HEADS = 4
XATTN_HEAD_DIM = 64
XATTN_WIDTH = XATTN_HEADS * XATTN_HEAD_DIM
TOK_WIDTH = MIX_WIDTH - XATTN_WIDTH
HEAD_DIM = 128
N_HEADS = TOK_WIDTH // HEAD_DIM
CHUNK = 128
SUB = 16
N_SUB = CHUNK // SUB
SPLIT_HALVES = (CHUNK // 2, CHUNK // 4, CHUNK // 8)
PAIR = 2 * HEAD_DIM
SEGMENTS = 8
SEGMENT_PAD = 8
POOL_WINDOWS = (2, 4, 8, 16)
POOL_GROUP = TOK_WIDTH // len(POOL_WINDOWS)
POOL_HALO = 16
CONV_WIDTH = 4
CONV_HALO = 8
LRU_C = 8.0
ALPHA = (2 * DEPTH) ** 0.25
LN_EPS = 1e-5
RMS_EPS = 1e-6

SEQ_TILE = 512
VMEM_LIMIT_BYTES = 56 * 1024 * 1024


def _dot(a, b):
    return jnp.dot(a.astype(BF16), b.astype(BF16), preferred_element_type=F32)


def _dot_nt(a, b):
    return lax.dot_general(a.astype(BF16), b.astype(BF16), (((1,), (1,)), ((), ())),
                           preferred_element_type=F32)


def _dot_tn(a, b):
    return lax.dot_general(a.astype(BF16), b.astype(BF16), (((0,), (0,)), ((), ())),
                           preferred_element_type=F32)


def _silu(t):
    return t * jax.nn.sigmoid(t)


def _kv_kernel(mem_ref, w_ref, kbd_ref, vo_ref):
    kv = jnp.dot(mem_ref[0].astype(BF16), w_ref[...], preferred_element_type=F32)
    kt = kv[:, :XATTN_WIDTH].T
    v = kv[:, XATTN_WIDTH:]
    row_head = lax.broadcasted_iota(jnp.int32, (XATTN_WIDTH, N_MEM), 0) // XATTN_HEAD_DIM
    lane_head = lax.broadcasted_iota(jnp.int32, (N_MEM, XATTN_WIDTH), 1) // XATTN_HEAD_DIM
    for h in range(XATTN_HEADS):
        kbd_ref[0, :, h * N_MEM:(h + 1) * N_MEM] = jnp.where(row_head == h, kt, 0.0).astype(BF16)
        vo_ref[0, h * N_MEM:(h + 1) * N_MEM, :XATTN_WIDTH] = (
            jnp.where(lane_head == h, v, 0.0).astype(BF16))
        vo_ref[0, h * N_MEM:(h + 1) * N_MEM, XATTN_WIDTH:] = (
            jnp.where(lane_head == h, 1.0, 0.0).astype(BF16))


def _memory_operands(mem, mem_kv_w):
    batch = mem.shape[0]
    return pl.pallas_call(
        _kv_kernel,
        grid=(batch,),
        in_specs=[pl.BlockSpec((1, N_MEM, D_MODEL), lambda b: (b, 0, 0)),
                  pl.BlockSpec((D_MODEL, 2 * XATTN_WIDTH), lambda b: (0, 0))],
        out_specs=[pl.BlockSpec((1, XATTN_WIDTH, XATTN_HEADS * N_MEM), lambda b: (b, 0, 0)),
                   pl.BlockSpec((1, XATTN_HEADS * N_MEM, 2 * XATTN_WIDTH), lambda b: (b, 0, 0))],
        out_shape=[jax.ShapeDtypeStruct((batch, XATTN_WIDTH, XATTN_HEADS * N_MEM), BF16),
                   jax.ShapeDtypeStruct((batch, XATTN_HEADS * N_MEM, 2 * XATTN_WIDTH), BF16)],
        name="memory_kv",
    )(mem, mem_kv_w.astype(BF16))


def _cross_attention(qx, kbd_ref, vo_ref):
    s = jnp.dot(qx.astype(BF16), kbd_ref[0], preferred_element_type=F32) * (XATTN_HEAD_DIM ** -0.5)
    probs = []
    for h in range(XATTN_HEADS):
        sh = s[:, h * N_MEM:(h + 1) * N_MEM]
        probs.append(jnp.exp(sh - jnp.max(sh, axis=-1, keepdims=True)).astype(BF16))
    nd = jnp.dot(jnp.concatenate(probs, axis=-1), vo_ref[0], preferred_element_type=F32)
    return nd[:, :XATTN_WIDTH] / nd[:, XATTN_WIDTH:]


def _layer_tail(x, tok, qx, gate, kbd_ref, vo_ref, wout_ref, lng_ref, lnb_ref, o_ref):
    xo = _cross_attention(qx, kbd_ref, vo_ref)
    mixed = jnp.concatenate([tok, xo], axis=-1) * _silu(gate)
    y = jnp.dot(mixed.astype(BF16), wout_ref[...], preferred_element_type=F32)
    z = ALPHA * x + y
    zc = z - jnp.mean(z, axis=-1, keepdims=True)
    var = jnp.mean(zc * zc, axis=-1, keepdims=True)
    o_ref[0] = zc * lax.rsqrt(var + LN_EPS) * lng_ref[...] + lnb_ref[...]


def _project(xb, win_ref, start, width):
    return jnp.dot(xb, win_ref[:, start:start + width], preferred_element_type=F32)


def _gmlp_layer_kernel(x_ref, win_ref, ws_ref, bs_ref, kbd_ref, vo_ref, wout_ref, lng_ref, lnb_ref,
                       o_ref, tok_ref):
    x = x_ref[0]
    xb = x.astype(BF16)
    u = jax.nn.gelu(_project(xb, win_ref, 0, TOK_WIDTH))
    v = jax.nn.gelu(_project(xb, win_ref, TOK_WIDTH, TOK_WIDTH))
    qx = _project(xb, win_ref, 2 * TOK_WIDTH, XATTN_WIDTH)
    gate = _project(xb, win_ref, 2 * TOK_WIDTH + XATTN_WIDTH, MIX_WIDTH)
    rows = lax.broadcasted_iota(jnp.int32, (CHUNK, CHUNK), 0)
    cols = lax.broadcasted_iota(jnp.int32, (CHUNK, CHUNK), 1)
    causal = cols <= rows
    for g in range(N_HEADS):
        lanes = slice(g * HEAD_DIM, (g + 1) * HEAD_DIM)
        vg = v[:, lanes]
        vc = vg - jnp.mean(vg, axis=-1, keepdims=True)
        vn = (vc * lax.rsqrt(jnp.mean(vc * vc, axis=-1, keepdims=True) + LN_EPS)).astype(BF16)
        w = jnp.where(causal, ws_ref[g], 0.0).astype(BF16)
        bias = bs_ref[:, g:g + 1]
        for c in range(x.shape[0] // CHUNK):
            r = slice(c * CHUNK, (c + 1) * CHUNK)
            spatial = jnp.dot(w, vn[r], preferred_element_type=F32) + bias
            tok_ref[r, lanes] = u[r, lanes] * spatial
    _layer_tail(x, tok_ref[...], qx, gate, kbd_ref, vo_ref, wout_ref, lng_ref, lnb_ref, o_ref)


def _block_diag2(t):
    zero = jnp.zeros_like(t[:, :HEAD_DIM])
    return jnp.concatenate([jnp.concatenate([t[:, :HEAD_DIM], zero], axis=1),
                            jnp.concatenate([zero, t[:, HEAD_DIM:]], axis=1)], axis=0)


def _scale_blocks(pick, e_ref, idx, lanes):
    return jnp.concatenate(
        [pick(j)[j * SUB:(j + 1) * SUB, lanes] * e_ref[idx, j:j + 1, lanes] for j in range(N_SUB)],
        axis=0).astype(BF16)


def _hgrn_layer_kernel(layer, x_ref, win_ref, lbl_ref, ng_ref, kbd_ref, vo_ref, wout_ref, lng_ref,
                       lnb_ref, o_ref, tok_ref, state_ref, gl_ref, s_ref, e_ref):
    @pl.when(pl.program_id(1) == 0)
    def _():
        state_ref[...] = jnp.zeros_like(state_ref)

    x = x_ref[0]
    xb = x.astype(BF16)
    q = _silu(_project(xb, win_ref, 0, TOK_WIDTH))
    f_logit = _project(xb, win_ref, TOK_WIDTH, TOK_WIDTH)
    val = _project(xb, win_ref, 2 * TOK_WIDTH, TOK_WIDTH)
    qx = _project(xb, win_ref, 3 * TOK_WIDTH, XATTN_WIDTH)
    gate = _project(xb, win_ref, 3 * TOK_WIDTH + XATTN_WIDTH, MIX_WIDTH)

    lbl = lbl_ref[...]
    lbe = jnp.exp(lbl - jnp.max(lbl, axis=0, keepdims=True))
    lbp = lbe / jnp.sum(lbe, axis=0, keepdims=True)
    lb = jnp.sum(lbp[:layer + 1], axis=0, keepdims=True) - lbp[0:1]
    f = lb + (1.0 - lb) * jax.nn.sigmoid(f_logit)
    log_f = jnp.log(f)
    k = 1.0 - f

    tt = lax.broadcasted_iota(jnp.int32, (CHUNK, PAIR), 0)
    ss = lax.broadcasted_iota(jnp.int32, (CHUNK, PAIR), 1) % CHUNK
    level_masks = [((tt // (2 * h)) == (ss // (2 * h))) & ((tt % (2 * h)) >= h) & ((ss % (2 * h)) < h)
                   for h in SPLIT_HALVES]
    diag_mask = ((tt // SUB) == (ss // SUB)) & (ss <= tt)
    br = lax.broadcasted_iota(jnp.int32, (CHUNK, CHUNK), 0)
    bc = lax.broadcasted_iota(jnp.int32, (CHUNK, CHUNK), 1)
    block_tril = jnp.where(((br // SUB) == (bc // SUB)) & (bc <= br), 1.0, 0.0).astype(BF16)
    sub_row = lax.broadcasted_iota(jnp.int32, (N_SUB, 1), 0)
    n_lvl = len(SPLIT_HALVES)

    for c in range(x.shape[0] // CHUNK):
        r = slice(c * CHUNK, (c + 1) * CHUNK)
        lfc = log_f[r]
        hi = lfc.astype(BF16)
        lo = (lfc - hi.astype(F32)).astype(BF16)
        gl = (jnp.dot(block_tril, hi, preferred_element_type=F32)
              + jnp.dot(block_tril, lo, preferred_element_type=F32))
        ql = q[r] * jnp.exp(gl)
        kl = k[r] * jnp.exp(-gl)

        for h in range(N_HEADS):
            gl_ref[h] = gl[:, h * HEAD_DIM:(h + 1) * HEAD_DIM]
        bsum = jnp.concatenate(
            [gl_ref[h, pl.ds(SUB - 1, N_SUB, stride=SUB), :] for h in range(N_HEADS)], axis=1)
        incl = bsum
        for d in (1, 2, 4):
            incl = incl + jnp.where(sub_row >= d, pltpu.roll(incl, d, 0), 0.0)
        s = incl - bsum
        s_ref[...] = s
        total = s_ref[N_SUB - 1:N_SUB, :] + gl[CHUNK - 1:CHUNK, :]
        for li, half in enumerate(SPLIT_HALVES):
            nb = half // SUB
            boundary = s_ref[nb:nb + 1, :]
            for g0 in range(2 * nb, N_SUB, 2 * nb):
                boundary = jnp.where(sub_row >= g0, s_ref[g0 + nb:g0 + nb + 1, :], boundary)
            e_ref[li] = jnp.exp(-jnp.abs(s - boundary))
        e_ref[n_lvl] = jnp.exp(s)
        e_ref[n_lvl + 1] = jnp.exp(total - s)
        decay_total = jnp.exp(total)

        for p in range(N_HEADS // 2):
            lanes = slice(p * PAIR, (p + 1) * PAIR)
            scores = jnp.where(diag_mask, _dot_nt(ql[:, lanes], _block_diag2(kl[:, lanes].astype(BF16))), 0.0)
            for li, half in enumerate(SPLIT_HALVES):
                nb = half // SUB
                xs = _scale_blocks(lambda j: ql if (j // nb) % 2 else kl, e_ref, li, lanes)
                scores = jnp.where(level_masks[li], _dot_nt(xs, _block_diag2(xs)), scores)
            q_dec = _scale_blocks(lambda j: ql, e_ref, n_lvl, lanes)
            k_end = _scale_blocks(lambda j: kl, e_ref, n_lvl + 1, lanes)
            vc = val[r, lanes].astype(BF16)
            heads = (2 * p, 2 * p + 1)
            states = [state_ref[h] for h in heads]
            zero = jnp.zeros((HEAD_DIM, HEAD_DIM), BF16)
            state_bd = jnp.concatenate(
                [jnp.concatenate([states[0].astype(BF16), zero], axis=1),
                 jnp.concatenate([zero, states[1].astype(BF16)], axis=1)], axis=0)
            o = _dot(scores, _block_diag2(vc)) + _dot_nt(q_dec, state_bd)
            for i, h in enumerate(heads):
                hl = slice(i * HEAD_DIM, (i + 1) * HEAD_DIM)
                head_lanes = slice(h * HEAD_DIM, (h + 1) * HEAD_DIM)
                state_ref[h] = states[i] * decay_total[:, head_lanes] + _dot_tn(vc[:, hl], k_end[:, hl])
                oh = o[:, hl]
                oh = oh * lax.rsqrt(jnp.mean(oh * oh, axis=-1, keepdims=True) + RMS_EPS)
                tok_ref[r, head_lanes] = oh * ng_ref[:, head_lanes]
    _layer_tail(x, tok_ref[...], qx, gate, kbd_ref, vo_ref, wout_ref, lng_ref, lnb_ref, o_ref)


def _pool_layer_kernel(x_ref, win_ref, wpool_ref, scale_ref, kbd_ref, vo_ref, wout_ref, lng_ref,
                       lnb_ref, o_ref, pbuf_ref):
    si = pl.program_id(1)
    tile = x_ref.shape[1]

    @pl.when(si == 0)
    def _():
        pbuf_ref[:POOL_HALO] = jnp.zeros((POOL_HALO, TOK_WIDTH), F32)

    x = x_ref[0]
    xb = x.astype(BF16)
    p = _project(xb, win_ref, 0, TOK_WIDTH)
    qx = _project(xb, win_ref, TOK_WIDTH, XATTN_WIDTH)
    gate = _project(xb, win_ref, TOK_WIDTH + XATTN_WIDTH, MIX_WIDTH)

    pbuf_ref[POOL_HALO:] = p
    ext = pbuf_ref[...]
    sums = []
    acc = ext
    for w in POOL_WINDOWS:
        acc = acc + pltpu.roll(acc, w // 2, 0)
        sums.append(acc[POOL_HALO:])
    pbuf_ref[:POOL_HALO] = p[tile - POOL_HALO:]

    pos = si * tile + lax.broadcasted_iota(jnp.int32, (tile, 1), 0)
    chan = lax.broadcasted_iota(jnp.int32, (1, TOK_WIDTH), 1)
    pooled = None
    for gi in reversed(range(len(POOL_WINDOWS))):
        w = POOL_WINDOWS[gi]
        mean = sums[gi] * (1.0 / jnp.minimum(pos + 1, w).astype(F32))
        pooled = mean if pooled is None else jnp.where(chan < (gi + 1) * POOL_GROUP, mean, pooled)
    tok = _dot(pooled - p, wpool_ref[...]) * scale_ref[...]
    _layer_tail(x, tok, qx, gate, kbd_ref, vo_ref, wout_ref, lng_ref, lnb_ref, o_ref)


def _lru_layer_kernel(x_ref, win_ref, convw_ref, convb_ref, wg_ref, bg_ref, ap_ref, kbd_ref, vo_ref,
                      wout_ref, lng_ref, lnb_ref, o_ref, cbuf_ref, a_ref, b_ref, carry_ref, segp_ref,
                      segh_ref):
    si = pl.program_id(1)
    tile = x_ref.shape[1]

    @pl.when(si == 0)
    def _():
        cbuf_ref[:CONV_HALO] = jnp.zeros((CONV_HALO, TOK_WIDTH), F32)
        carry_ref[...] = jnp.zeros_like(carry_ref)

    x = x_ref[0]
    xb = x.astype(BF16)
    xin = _project(xb, win_ref, 0, TOK_WIDTH)
    qx = _project(xb, win_ref, TOK_WIDTH, XATTN_WIDTH)
    gate = _project(xb, win_ref, TOK_WIDTH + XATTN_WIDTH, MIX_WIDTH)

    cbuf_ref[CONV_HALO:] = xin
    ext = cbuf_ref[...]
    conv = convw_ref[CONV_WIDTH - 1:CONV_WIDTH] * ext
    for j in range(1, CONV_WIDTH):
        conv = conv + convw_ref[CONV_WIDTH - 1 - j:CONV_WIDTH - j] * pltpu.roll(ext, j, 0)
    xc = conv[CONV_HALO:] + convb_ref[...]
    cbuf_ref[:CONV_HALO] = xin[tile - CONV_HALO:]

    seg = tile // SEGMENTS
    seg_stride = seg + SEGMENT_PAD
    ap = ap_ref[...]
    neg_softplus = -(jnp.maximum(-ap, 0.0) + jnp.log1p(jnp.exp(-jnp.abs(ap))))
    first = (si * tile + lax.broadcasted_iota(jnp.int32, (tile, 1), 0)) == 0
    for h in range(N_HEADS):
        lanes = slice(h * HEAD_DIM, (h + 1) * HEAD_DIM)
        xh = xc[:, lanes]
        gates = jax.nn.sigmoid(_dot(xh, wg_ref[h]) + bg_ref[h])
        log_a = LRU_C * gates[:, HEAD_DIM:] * neg_softplus[:, lanes]
        th = jnp.tanh(-log_a)
        mult = jnp.where(first, 1.0, jnp.sqrt(2.0 * th / (1.0 + th)))
        a_val = jnp.exp(log_a)
        b_val = mult * gates[:, :HEAD_DIM] * xh
        for s in range(SEGMENTS):
            a_ref[h, s * seg_stride:s * seg_stride + seg] = a_val[s * seg:(s + 1) * seg]
            b_ref[h, s * seg_stride:s * seg_stride + seg] = b_val[s * seg:(s + 1) * seg]

    rows = [pl.ds(r, SEGMENTS, stride=seg_stride) for r in range(seg)]
    for h in range(N_HEADS):
        acc = jnp.zeros((SEGMENTS, HEAD_DIM), F32)
        prod = jnp.ones((SEGMENTS, HEAD_DIM), F32)
        for r in rows:
            ar = a_ref[h, r, :]
            acc = ar * acc + b_ref[h, r, :]
            prod = prod * ar
        segp_ref[h] = prod
        segh_ref[h] = acc
        for s in range(1, SEGMENTS):
            carry_ref[h, s:s + 1] = (segp_ref[h, s - 1:s] * carry_ref[h, s - 1:s] + segh_ref[h, s - 1:s])
        acc = carry_ref[h]
        for r in rows:
            acc = a_ref[h, r, :] * acc + b_ref[h, r, :]
            b_ref[h, r, :] = acc
        last = (SEGMENTS - 1) * seg_stride + seg - 1
        carry_ref[h, 0:1] = b_ref[h, last:last + 1]
    hseq = jnp.concatenate(
        [jnp.concatenate([b_ref[h, s * seg_stride:s * seg_stride + seg] for s in range(SEGMENTS)], axis=0)
         for h in range(N_HEADS)], axis=1)
    _layer_tail(x, hseq, qx, gate, kbd_ref, vo_ref, wout_ref, lng_ref, lnb_ref, o_ref)


def _full(arr):
    nd = arr.ndim
    return pl.BlockSpec(arr.shape, lambda b, s: (0,) * nd)


def _run_layer(body, x, params, kbd, vo, w_out, ln_g, ln_b, scratch, name):
    batch, seq, _ = x.shape
    tile = SEQ_TILE
    x_spec = pl.BlockSpec((1, tile, D_MODEL), lambda b, s: (b, s, 0))
    shared = [w_out.astype(BF16), ln_g.reshape(1, D_MODEL), ln_b.reshape(1, D_MODEL)]
    in_specs = ([x_spec] + [_full(p) for p in params]
                + [pl.BlockSpec((1,) + kbd.shape[1:], lambda b, s: (b, 0, 0)),
                   pl.BlockSpec((1,) + vo.shape[1:], lambda b, s: (b, 0, 0))]
                + [_full(p) for p in shared])
    return pl.pallas_call(
        body,
        grid=(batch, seq // tile),
        in_specs=in_specs,
        out_specs=x_spec,
        out_shape=jax.ShapeDtypeStruct(x.shape, x.dtype),
        scratch_shapes=scratch,
        compiler_params=pltpu.CompilerParams(
            dimension_semantics=("arbitrary", "arbitrary"), vmem_limit_bytes=VMEM_LIMIT_BYTES),
        name=name,
    )(x, *params, kbd, vo, *shared)


def kernel(x, mem, mem_kv_w, ln_g, ln_b, w_out, hgrn_lb_logits, a_w_in, a_w_s, a_b_s, b_w_in, b_norm_g,
           c_w_in, c_w_pool, c_scale, d_w_in, d_conv_w, d_conv_b, d_w_gx, d_b_gx, d_w_ga, d_b_ga,
           d_a_param):
    tile = SEQ_TILE
    assert x.shape[1] % tile == 0 and tile % CHUNK == 0
    kbd, vo = _memory_operands(mem, mem_kv_w)
    tok_scratch = pltpu.VMEM((tile, TOK_WIDTH), F32)
    for i in range(DEPTH):
        kind, j = i % 4, i // 4
        if kind == 0:
            params = [a_w_in[j].astype(BF16), a_w_s[j], a_b_s[j].T]
            body, scratch = _gmlp_layer_kernel, [tok_scratch]
        elif kind == 1:
            params = [b_w_in[j].astype(BF16), hgrn_lb_logits, b_norm_g[j].reshape(1, TOK_WIDTH)]
            body = functools.partial(_hgrn_layer_kernel, i)
            scratch = [tok_scratch, pltpu.VMEM((N_HEADS, HEAD_DIM, HEAD_DIM), F32),
                       pltpu.VMEM((N_HEADS, CHUNK, HEAD_DIM), F32), pltpu.VMEM((N_SUB, TOK_WIDTH), F32),
                       pltpu.VMEM((len(SPLIT_HALVES) + 2, N_SUB, TOK_WIDTH), F32)]
        elif kind == 2:
            wpool = jax.scipy.linalg.block_diag(*[c_w_pool[j, g] for g in range(len(POOL_WINDOWS))])
            params = [c_w_in[j].astype(BF16), wpool.astype(BF16), c_scale[j].reshape(1, TOK_WIDTH)]
            body = _pool_layer_kernel
            scratch = [pltpu.VMEM((POOL_HALO + tile, TOK_WIDTH), F32)]
        else:
            wg = jnp.concatenate([d_w_gx[j], d_w_ga[j]], axis=-1).astype(BF16)
            bg = jnp.concatenate([d_b_gx[j], d_b_ga[j]], axis=-1)[:, None, :]
            params = [d_w_in[j].astype(BF16), d_conv_w[j], d_conv_b[j].reshape(1, TOK_WIDTH), wg, bg,
                      d_a_param[j].reshape(1, TOK_WIDTH)]
            body = _lru_layer_kernel
            scratch = [pltpu.VMEM((CONV_HALO + tile, TOK_WIDTH), F32),
                       pltpu.VMEM((N_HEADS, tile + SEGMENTS * SEGMENT_PAD, HEAD_DIM), F32),
                       pltpu.VMEM((N_HEADS, tile + SEGMENTS * SEGMENT_PAD, HEAD_DIM), F32),
                       pltpu.VMEM((N_HEADS, SEGMENTS, HEAD_DIM), F32),
                       pltpu.VMEM((N_HEADS, SEGMENTS, HEAD_DIM), F32),
                       pltpu.VMEM((N_HEADS, SEGMENTS, HEAD_DIM), F32)]
        x = _run_layer(body, x, params, kbd, vo, w_out[i], ln_g[i], ln_b[i], scratch,
                       name=f"layer{i}")
    return x
```

```python
import functools

import jax
import jax.numpy as jnp
from jax import lax
from jax.experimental import pallas as pl
from jax.experimental.pallas import tpu as pltpu

F32 = jnp.float32
BF16 = jnp.bfloat16

D_MODEL = 1024
DEPTH = 4
N_MEM = 256
MIX_WIDTH = D_MODEL
XATTN_HEADS = 4
XATTN_HEAD_DIM = 64
XATTN_WIDTH = XATTN_HEADS * XATTN_HEAD_DIM
TOK_WIDTH = MIX_WIDTH - XATTN_WIDTH
HEAD_DIM = 128
N_HEADS = TOK_WIDTH // HEAD_DIM
CHUNK = 128
SUB = 16
N_SUB = CHUNK // SUB
SPLIT_HALVES = (CHUNK // 2, CHUNK // 4, CHUNK // 8)
PAIR = 2 * HEAD_DIM
SEGMENTS = 8
SEGMENT_PAD = 8
POOL_WINDOWS = (2, 4, 8, 16)
POOL_GROUP = TOK_WIDTH // len(POOL_WINDOWS)
POOL_HALO = 16
CONV_WIDTH = 4
CONV_HALO = 8
LRU_C = 8.0
ALPHA = (2 * DEPTH) ** 0.25
LN_EPS = 1e-5
RMS_EPS = 1e-6

SEQ_TILE = 512
PROJ_PIECE = 256
TAIL_ROWS = 256
VMEM_LIMIT_BYTES = 56 * 1024 * 1024


def _dot(a, b):
    return jnp.dot(a.astype(BF16), b.astype(BF16), preferred_element_type=F32)


def _dot_nt(a, b):
    return lax.dot_general(a.astype(BF16), b.astype(BF16), (((1,), (1,)), ((), ())),
                           preferred_element_type=F32)


def _dot_tn(a, b):
    return lax.dot_general(a.astype(BF16), b.astype(BF16), (((0,), (0,)), ((), ())),
                           preferred_element_type=F32)


def _silu(t):
    return t * jax.nn.sigmoid(t)


def _kv_kernel(mem_ref, w_ref, kbd_ref, vo_ref):
    kv = jnp.dot(mem_ref[0].astype(BF16), w_ref[...], preferred_element_type=F32)
    kt = kv[:, :XATTN_WIDTH].T
    v = kv[:, XATTN_WIDTH:]
    row_head = lax.broadcasted_iota(jnp.int32, (XATTN_WIDTH, N_MEM), 0) // XATTN_HEAD_DIM
    lane_head = lax.broadcasted_iota(jnp.int32, (N_MEM, XATTN_WIDTH), 1) // XATTN_HEAD_DIM
    for h in range(XATTN_HEADS):
        kbd_ref[0, :, h * N_MEM:(h + 1) * N_MEM] = jnp.where(row_head == h, kt, 0.0).astype(BF16)
        vo_ref[0, h * N_MEM:(h + 1) * N_MEM, :XATTN_WIDTH] = (
            jnp.where(lane_head == h, v, 0.0).astype(BF16))
        vo_ref[0, h * N_MEM:(h + 1) * N_MEM, XATTN_WIDTH:] = (
            jnp.where(lane_head == h, 1.0, 0.0).astype(BF16))


def _memory_operands(mem, mem_kv_w):
    batch = mem.shape[0]
    return pl.pallas_call(
        _kv_kernel,
        grid=(batch,),
        in_specs=[pl.BlockSpec((1, N_MEM, D_MODEL), lambda b: (b, 0, 0)),
                  pl.BlockSpec((D_MODEL, 2 * XATTN_WIDTH), lambda b: (0, 0))],
        out_specs=[pl.BlockSpec((1, XATTN_WIDTH, XATTN_HEADS * N_MEM), lambda b: (b, 0, 0)),
                   pl.BlockSpec((1, XATTN_HEADS * N_MEM, 2 * XATTN_WIDTH), lambda b: (b, 0, 0))],
        out_shape=[jax.ShapeDtypeStruct((batch, XATTN_WIDTH, XATTN_HEADS * N_MEM), BF16),
                   jax.ShapeDtypeStruct((batch, XATTN_HEADS * N_MEM, 2 * XATTN_WIDTH), BF16)],
        name="memory_kv",
    )(mem, mem_kv_w.astype(BF16))


def _cross_attention(qx, kbd_ref, vo_ref):
    s = jnp.dot(qx.astype(BF16), kbd_ref[0], preferred_element_type=F32) * (XATTN_HEAD_DIM ** -0.5)
    probs = []
    for h in range(XATTN_HEADS):
        sh = s[:, h * N_MEM:(h + 1) * N_MEM]
        probs.append(jnp.exp(sh - jnp.max(sh, axis=-1, keepdims=True)).astype(BF16))
    nd = jnp.dot(jnp.concatenate(probs, axis=-1), vo_ref[0], preferred_element_type=F32)
    return nd[:, :XATTN_WIDTH] / nd[:, XATTN_WIDTH:]


def _layer_tail(x_ref, tok_ref, proj_ref, qx_start, kbd_ref, vo_ref, wout_ref, lng_ref, lnb_ref, o_ref):
    gate_start = qx_start + XATTN_WIDTH
    for r0 in range(0, x_ref.shape[1], TAIL_ROWS):
        r = slice(r0, r0 + TAIL_ROWS)
        xo = _cross_attention(proj_ref[r, qx_start:gate_start], kbd_ref, vo_ref)
        mixed = jnp.concatenate([tok_ref[r, :], xo], axis=-1) * _silu(proj_ref[r, gate_start:gate_start + MIX_WIDTH])
        y = jnp.dot(mixed.astype(BF16), wout_ref[...], preferred_element_type=F32)
        z = ALPHA * x_ref[0, r, :] + y
        zc = z - jnp.mean(z, axis=-1, keepdims=True)
        var = jnp.mean(zc * zc, axis=-1, keepdims=True)
        o_ref[0, r, :] = zc * lax.rsqrt(var + LN_EPS) * lng_ref[...] + lnb_ref[...]


class _Projector:
    def __init__(self, x_ref, win_ref, proj_ref, xb_ref):
        xb_ref[...] = x_ref[0].astype(BF16)
        self._refs = (win_ref, proj_ref, xb_ref)
        self._starts = list(range(0, proj_ref.shape[1], PROJ_PIECE))

    def step(self, count=1):
        win_ref, proj_ref, xb_ref = self._refs
        for _ in range(min(count, len(self._starts))):
            c = self._starts.pop(0)
            w = min(PROJ_PIECE, proj_ref.shape[1] - c)
            proj_ref[:, c:c + w] = jnp.dot(xb_ref[...], win_ref[:, c:c + w], preferred_element_type=F32)

    def through(self, column):
        self.step(sum(1 for c in self._starts if c < column))

    def rest(self):
        self.step(len(self._starts))


def _columns(proj, start, width):
    return proj[:, start:start + width]


def _gmlp_layer_kernel(x_ref, win_ref, ws_ref, bs_ref, kbd_ref, vo_ref, wout_ref, lng_ref, lnb_ref, o_ref,
                       proj, xb_ref, tok_ref):
    pieces = _Projector(x_ref, win_ref, proj, xb_ref)
    pieces.through(TOK_WIDTH)
    u = jax.nn.gelu(_columns(proj, 0, TOK_WIDTH))
    pieces.through(2 * TOK_WIDTH)
    v = jax.nn.gelu(_columns(proj, TOK_WIDTH, TOK_WIDTH))
    rows = lax.broadcasted_iota(jnp.int32, (CHUNK, CHUNK), 0)
    cols = lax.broadcasted_iota(jnp.int32, (CHUNK, CHUNK), 1)
    causal = cols <= rows
    for g in range(N_HEADS):
        lanes = slice(g * HEAD_DIM, (g + 1) * HEAD_DIM)
        vg = v[:, lanes]
        vc = vg - jnp.mean(vg, axis=-1, keepdims=True)
        vn = (vc * lax.rsqrt(jnp.mean(vc * vc, axis=-1, keepdims=True) + LN_EPS)).astype(BF16)
        w = jnp.where(causal, ws_ref[g], 0.0).astype(BF16)
        bias = bs_ref[:, g:g + 1]
        pieces.step()
        for c in range(x_ref.shape[1] // CHUNK):
            r = slice(c * CHUNK, (c + 1) * CHUNK)
            spatial = jnp.dot(w, vn[r], preferred_element_type=F32) + bias
            tok_ref[r, lanes] = u[r, lanes] * spatial
    pieces.rest()
    _layer_tail(x_ref, tok_ref, proj, 2 * TOK_WIDTH, kbd_ref, vo_ref, wout_ref, lng_ref, lnb_ref, o_ref)


def _block_diag2(t):
    zero = jnp.zeros_like(t[:, :HEAD_DIM])
    return jnp.concatenate([jnp.concatenate([t[:, :HEAD_DIM], zero], axis=1),
                            jnp.concatenate([zero, t[:, HEAD_DIM:]], axis=1)], axis=0)


def _scale_blocks(pick, e_ref, idx, lanes):
    return jnp.concatenate(
        [pick(j)[j * SUB:(j + 1) * SUB, lanes] * e_ref[idx, j:j + 1, lanes] for j in range(N_SUB)],
        axis=0).astype(BF16)


def _hgrn_layer_kernel(layer, x_ref, win_ref, lbl_ref, ng_ref, kbd_ref, vo_ref, wout_ref, lng_ref, lnb_ref,
                       o_ref, proj, xb_ref, tok_ref, state_ref, gl_ref, s_ref, e_ref):
    @pl.when(pl.program_id(1) == 0)
    def _():
        state_ref[...] = jnp.zeros_like(state_ref)

    pieces = _Projector(x_ref, win_ref, proj, xb_ref)
    pieces.through(TOK_WIDTH)
    q = _silu(_columns(proj, 0, TOK_WIDTH))
    pieces.through(2 * TOK_WIDTH)
    f_logit = _columns(proj, TOK_WIDTH, TOK_WIDTH)

    lbl = lbl_ref[...]
    lbe = jnp.exp(lbl - jnp.max(lbl, axis=0, keepdims=True))
    lbp = lbe / jnp.sum(lbe, axis=0, keepdims=True)
    lb = jnp.sum(lbp[:layer + 1], axis=0, keepdims=True) - lbp[0:1]
    f = lb + (1.0 - lb) * jax.nn.sigmoid(f_logit)
    pieces.through(3 * TOK_WIDTH)
    log_f = jnp.log(f)
    k = 1.0 - f
    val = _columns(proj, 2 * TOK_WIDTH, TOK_WIDTH)

    tt = lax.broadcasted_iota(jnp.int32, (CHUNK, PAIR), 0)
    ss = lax.broadcasted_iota(jnp.int32, (CHUNK, PAIR), 1) % CHUNK
    level_masks = [((tt // (2 * h)) == (ss // (2 * h))) & ((tt % (2 * h)) >= h) & ((ss % (2 * h)) < h)
                   for h in SPLIT_HALVES]
    diag_mask = ((tt // SUB) == (ss // SUB)) & (ss <= tt)
    br = lax.broadcasted_iota(jnp.int32, (CHUNK, CHUNK), 0)
    bc = lax.broadcasted_iota(jnp.int32, (CHUNK, CHUNK), 1)
    block_tril = jnp.where(((br // SUB) == (bc // SUB)) & (bc <= br), 1.0, 0.0).astype(BF16)
    sub_row = lax.broadcasted_iota(jnp.int32, (N_SUB, 1), 0)
    n_lvl = len(SPLIT_HALVES)

    for c in range(x_ref.shape[1] // CHUNK):
        r = slice(c * CHUNK, (c + 1) * CHUNK)
        lfc = log_f[r]
        hi = lfc.astype(BF16)
        lo = (lfc - hi.astype(F32)).astype(BF16)
        gl = (jnp.dot(block_tril, hi, preferred_element_type=F32)
              + jnp.dot(block_tril, lo, preferred_element_type=F32))
        ql = q[r] * jnp.exp(gl)
        kl = k[r] * jnp.exp(-gl)

        for h in range(N_HEADS):
            gl_ref[h] = gl[:, h * HEAD_DIM:(h + 1) * HEAD_DIM]
        bsum = jnp.concatenate(
            [gl_ref[h, pl.ds(SUB - 1, N_SUB, stride=SUB), :] for h in range(N_HEADS)], axis=1)
        incl = bsum
        for d in (1, 2, 4):
            incl = incl + jnp.where(sub_row >= d, pltpu.roll(incl, d, 0), 0.0)
        s = incl - bsum
        s_ref[...] = s
        total = s_ref[N_SUB - 1:N_SUB, :] + gl[CHUNK - 1:CHUNK, :]
        for li, half in enumerate(SPLIT_HALVES):
            nb = half // SUB
            boundary = s_ref[nb:nb + 1, :]
            for g0 in range(2 * nb, N_SUB, 2 * nb):
                boundary = jnp.where(sub_row >= g0, s_ref[g0 + nb:g0 + nb + 1, :], boundary)
            e_ref[li] = jnp.exp(-jnp.abs(s - boundary))
        e_ref[n_lvl] = jnp.exp(s)
        e_ref[n_lvl + 1] = jnp.exp(total - s)
        decay_total = jnp.exp(total)
        pieces.step(2)

        for p in range(N_HEADS // 2):
            lanes = slice(p * PAIR, (p + 1) * PAIR)
            scores = jnp.where(diag_mask, _dot_nt(ql[:, lanes], _block_diag2(kl[:, lanes].astype(BF16))), 0.0)
            for li, half in enumerate(SPLIT_HALVES):
                nb = half // SUB
                xs = _scale_blocks(lambda j: ql if (j // nb) % 2 else kl, e_ref, li, lanes)
                scores = jnp.where(level_masks[li], _dot_nt(xs, _block_diag2(xs)), scores)
            q_dec = _scale_blocks(lambda j: ql, e_ref, n_lvl, lanes)
            k_end = _scale_blocks(lambda j: kl, e_ref, n_lvl + 1, lanes)
            vc = val[r, lanes].astype(BF16)
            heads = (2 * p, 2 * p + 1)
            states = [state_ref[h] for h in heads]
            zero = jnp.zeros((HEAD_DIM, HEAD_DIM), BF16)
            state_bd = jnp.concatenate(
                [jnp.concatenate([states[0].astype(BF16), zero], axis=1),
                 jnp.concatenate([zero, states[1].astype(BF16)], axis=1)], axis=0)
            o = _dot(scores, _block_diag2(vc)) + _dot_nt(q_dec, state_bd)
            for i, h in enumerate(heads):
                hl = slice(i * HEAD_DIM, (i + 1) * HEAD_DIM)
                head_lanes = slice(h * HEAD_DIM, (h + 1) * HEAD_DIM)
                state_ref[h] = states[i] * decay_total[:, head_lanes] + _dot_tn(vc[:, hl], k_end[:, hl])
                oh = o[:, hl]
                oh = oh * lax.rsqrt(jnp.mean(oh * oh, axis=-1, keepdims=True) + RMS_EPS)
                tok_ref[r, head_lanes] = oh * ng_ref[:, head_lanes]
    pieces.rest()
    _layer_tail(x_ref, tok_ref, proj, 3 * TOK_WIDTH, kbd_ref, vo_ref, wout_ref, lng_ref, lnb_ref, o_ref)


def _pool_layer_kernel(x_ref, win_ref, wpool_ref, scale_ref, kbd_ref, vo_ref, wout_ref, lng_ref, lnb_ref, o_ref,
                       proj, xb_ref, tok_ref, pbuf_ref):
    si = pl.program_id(1)
    tile = x_ref.shape[1]

    @pl.when(si == 0)
    def _():
        pbuf_ref[:POOL_HALO] = jnp.zeros((POOL_HALO, TOK_WIDTH), F32)

    pieces = _Projector(x_ref, win_ref, proj, xb_ref)
    pieces.through(TOK_WIDTH)
    p = _columns(proj, 0, TOK_WIDTH)

    pbuf_ref[POOL_HALO:] = p
    ext = pbuf_ref[...]
    pieces.step(2)
    sums = []
    acc = ext
    for w in POOL_WINDOWS:
        acc = acc + pltpu.roll(acc, w // 2, 0)
        sums.append(acc[POOL_HALO:])
    pbuf_ref[:POOL_HALO] = p[tile - POOL_HALO:]

    pos = si * tile + lax.broadcasted_iota(jnp.int32, (tile, 1), 0)
    chan = lax.broadcasted_iota(jnp.int32, (1, TOK_WIDTH), 1)
    pooled = None
    for gi in reversed(range(len(POOL_WINDOWS))):
        w = POOL_WINDOWS[gi]
        mean = sums[gi] * (1.0 / jnp.minimum(pos + 1, w).astype(F32))
        pooled = mean if pooled is None else jnp.where(chan < (gi + 1) * POOL_GROUP, mean, pooled)
    pieces.rest()
    tok_ref[...] = _dot(pooled - p, wpool_ref[...]) * scale_ref[...]
    _layer_tail(x_ref, tok_ref, proj, TOK_WIDTH, kbd_ref, vo_ref, wout_ref, lng_ref, lnb_ref, o_ref)


def _lru_layer_kernel(x_ref, win_ref, convw_ref, convb_ref, wg_ref, bg_ref, ap_ref, kbd_ref, vo_ref, wout_ref,
                      lng_ref, lnb_ref, o_ref, proj, xb_ref, tok_ref, cbuf_ref, a_ref, b_ref, carry_ref,
                      segp_ref, segh_ref):
    si = pl.program_id(1)
    tile = x_ref.shape[1]

    @pl.when(si == 0)
    def _():
        cbuf_ref[:CONV_HALO] = jnp.zeros((CONV_HALO, TOK_WIDTH), F32)
        carry_ref[...] = jnp.zeros_like(carry_ref)

    pieces = _Projector(x_ref, win_ref, proj, xb_ref)
    pieces.through(TOK_WIDTH)
    xin = _columns(proj, 0, TOK_WIDTH)

    cbuf_ref[CONV_HALO:] = xin
    ext = cbuf_ref[...]
    conv = convw_ref[CONV_WIDTH - 1:CONV_WIDTH] * ext
    for j in range(1, CONV_WIDTH):
        conv = conv + convw_ref[CONV_WIDTH - 1 - j:CONV_WIDTH - j] * pltpu.roll(ext, j, 0)
    xc = conv[CONV_HALO:] + convb_ref[...]
    cbuf_ref[:CONV_HALO] = xin[tile - CONV_HALO:]

    seg = tile // SEGMENTS
    seg_stride = seg + SEGMENT_PAD
    ap = ap_ref[...]
    neg_softplus = -(jnp.maximum(-ap, 0.0) + jnp.log1p(jnp.exp(-jnp.abs(ap))))
    first = (si * tile + lax.broadcasted_iota(jnp.int32, (tile, 1), 0)) == 0
    for h in range(N_HEADS):
        lanes = slice(h * HEAD_DIM, (h + 1) * HEAD_DIM)
        pieces.step()
        xh = xc[:, lanes]
        gates = jax.nn.sigmoid(_dot(xh, wg_ref[h]) + bg_ref[h])
        log_a = LRU_C * gates[:, HEAD_DIM:] * neg_softplus[:, lanes]
        th = jnp.tanh(-log_a)
        mult = jnp.where(first, 1.0, jnp.sqrt(2.0 * th / (1.0 + th)))
        a_val = jnp.exp(log_a)
        b_val = mult * gates[:, :HEAD_DIM] * xh
        for s in range(SEGMENTS):
            a_ref[h, s * seg_stride:s * seg_stride + seg] = a_val[s * seg:(s + 1) * seg]
            b_ref[h, s * seg_stride:s * seg_stride + seg] = b_val[s * seg:(s + 1) * seg]

    rows = [pl.ds(r, SEGMENTS, stride=seg_stride) for r in range(seg)]
    for h in range(N_HEADS):
        acc = jnp.zeros((SEGMENTS, HEAD_DIM), F32)
        prod = jnp.ones((SEGMENTS, HEAD_DIM), F32)
        for r in rows:
            ar = a_ref[h, r, :]
            acc = ar * acc + b_ref[h, r, :]
            prod = prod * ar
        segp_ref[h] = prod
        segh_ref[h] = acc
        for s in range(1, SEGMENTS):
            carry_ref[h, s:s + 1] = (segp_ref[h, s - 1:s] * carry_ref[h, s - 1:s] + segh_ref[h, s - 1:s])
        acc = carry_ref[h]
        for r in rows:
            acc = a_ref[h, r, :] * acc + b_ref[h, r, :]
            b_ref[h, r, :] = acc
        last = (SEGMENTS - 1) * seg_stride + seg - 1
        carry_ref[h, 0:1] = b_ref[h, last:last + 1]
        for s in range(SEGMENTS):
            tok_ref[s * seg:(s + 1) * seg, h * HEAD_DIM:(h + 1) * HEAD_DIM] = (
                b_ref[h, s * seg_stride:s * seg_stride + seg])
    pieces.rest()
    _layer_tail(x_ref, tok_ref, proj, TOK_WIDTH, kbd_ref, vo_ref, wout_ref, lng_ref, lnb_ref, o_ref)


def _full(arr):
    nd = arr.ndim
    return pl.BlockSpec(arr.shape, lambda b, s: (0,) * nd, pipeline_mode=pl.Buffered(1))


def _run_layer(body, x, params, kbd, vo, w_out, ln_g, ln_b, scratch, name):
    batch, seq, _ = x.shape
    tile = SEQ_TILE
    x_spec = pl.BlockSpec((1, tile, D_MODEL), lambda b, s: (b, s, 0))
    shared = [w_out.astype(BF16), ln_g.reshape(1, D_MODEL), ln_b.reshape(1, D_MODEL)]
    in_specs = ([x_spec] + [_full(p) for p in params]
                + [pl.BlockSpec((1,) + kbd.shape[1:], lambda b, s: (b, 0, 0)),
                   pl.BlockSpec((1,) + vo.shape[1:], lambda b, s: (b, 0, 0))]
                + [_full(p) for p in shared])
    common = [pltpu.VMEM((tile, params[0].shape[1]), F32),
              pltpu.VMEM((tile, D_MODEL), BF16),
              pltpu.VMEM((tile, TOK_WIDTH), F32)]
    return pl.pallas_call(
        body,
        grid=(batch, seq // tile),
        in_specs=in_specs,
        out_specs=x_spec,
        out_shape=jax.ShapeDtypeStruct(x.shape, x.dtype),
        scratch_shapes=common + scratch,
        compiler_params=pltpu.CompilerParams(
            dimension_semantics=("arbitrary", "arbitrary"), vmem_limit_bytes=VMEM_LIMIT_BYTES),
        name=name,
    )(x, *params, kbd, vo, *shared)


def kernel(x, mem, mem_kv_w, ln_g, ln_b, w_out, hgrn_lb_logits, a_w_in, a_w_s, a_b_s, b_w_in, b_norm_g,
           c_w_in, c_w_pool, c_scale, d_w_in, d_conv_w, d_conv_b, d_w_gx, d_b_gx, d_w_ga, d_b_ga,
           d_a_param):
    tile = SEQ_TILE
    assert x.shape[1] % tile == 0 and tile % CHUNK == 0
    kbd, vo = _memory_operands(mem, mem_kv_w)
    for i in range(DEPTH):
        kind, j = i % 4, i // 4
        if kind == 0:
            params = [a_w_in[j].astype(BF16), a_w_s[j], a_b_s[j].T]
            body, scratch = _gmlp_layer_kernel, []
        elif kind == 1:
            params = [b_w_in[j].astype(BF16), hgrn_lb_logits, b_norm_g[j].reshape(1, TOK_WIDTH)]
            body = functools.partial(_hgrn_layer_kernel, i)
            scratch = [pltpu.VMEM((N_HEADS, HEAD_DIM, HEAD_DIM), F32),
                       pltpu.VMEM((N_HEADS, CHUNK, HEAD_DIM), F32), pltpu.VMEM((N_SUB, TOK_WIDTH), F32),
                       pltpu.VMEM((len(SPLIT_HALVES) + 2, N_SUB, TOK_WIDTH), F32)]
        elif kind == 2:
            wpool = jax.scipy.linalg.block_diag(*[c_w_pool[j, g] for g in range(len(POOL_WINDOWS))])
            params = [c_w_in[j].astype(BF16), wpool.astype(BF16), c_scale[j].reshape(1, TOK_WIDTH)]
            body = _pool_layer_kernel
            scratch = [pltpu.VMEM((POOL_HALO + tile, TOK_WIDTH), F32)]
        else:
            wg = jnp.concatenate([d_w_gx[j], d_w_ga[j]], axis=-1).astype(BF16)
            bg = jnp.concatenate([d_b_gx[j], d_b_ga[j]], axis=-1)[:, None, :]
            params = [d_w_in[j].astype(BF16), d_conv_w[j], d_conv_b[j].reshape(1, TOK_WIDTH), wg, bg,
                      d_a_param[j].reshape(1, TOK_WIDTH)]
            body = _lru_layer_kernel
            scratch = [pltpu.VMEM((CONV_HALO + tile, TOK_WIDTH), F32),
                       pltpu.VMEM((N_HEADS, tile + SEGMENTS * SEGMENT_PAD, HEAD_DIM), F32),
                       pltpu.VMEM((N_HEADS, tile + SEGMENTS * SEGMENT_PAD, HEAD_DIM), F32),
                       pltpu.VMEM((N_HEADS, SEGMENTS, HEAD_DIM), F32),
                       pltpu.VMEM((N_HEADS, SEGMENTS, HEAD_DIM), F32),
                       pltpu.VMEM((N_HEADS, SEGMENTS, HEAD_DIM), F32)]
        x = _run_layer(body, x, params, kbd, vo, w_out[i], ln_g[i], ln_b[i], scratch,
                       name=f"layer{i}")
    return x
```

```python
import functools
import math

import jax
import jax.numpy as jnp
from jax import lax
from jax.experimental import pallas as pl
from jax.experimental.pallas import tpu as pltpu

F32 = jnp.float32
BF16 = jnp.bfloat16

D_MODEL = 1024
DEPTH = 4
N_MEM = 256
MIX_WIDTH = D_MODEL
XATTN_HEADS = 4
XATTN_HEAD_DIM = 64
XATTN_WIDTH = XATTN_HEADS * XATTN_HEAD_DIM
XATTN_SCALE = XATTN_HEAD_DIM ** -0.5
assert math.frexp(XATTN_SCALE)[0] == 0.5
TOK_WIDTH = MIX_WIDTH - XATTN_WIDTH
HEAD_DIM = 128
N_HEADS = TOK_WIDTH // HEAD_DIM
CHUNK = 128
SUB = 16
N_SUB = CHUNK // SUB
SPLIT_HALVES = (CHUNK // 2, CHUNK // 4, CHUNK // 8)
PAIR = 2 * HEAD_DIM
SEGMENTS = 8
SEGMENT_PAD = 8
POOL_WINDOWS = (2, 4, 8, 16)
POOL_GROUP = TOK_WIDTH // len(POOL_WINDOWS)
POOL_HALO = 16
CONV_WIDTH = 4
CONV_HALO = 8
LRU_C = 8.0
ALPHA = (2 * DEPTH) ** 0.25
LN_EPS = 1e-5
RMS_EPS = 1e-6

SEQ_TILE = 512
PROJ_PIECE = 256
TAIL_ROWS = 256
VMEM_LIMIT_BYTES = 56 * 1024 * 1024


def _dot(a, b):
    return jnp.dot(a.astype(BF16), b.astype(BF16), preferred_element_type=F32)


def _dot_nt(a, b):
    return lax.dot_general(a.astype(BF16), b.astype(BF16), (((1,), (1,)), ((), ())),
                           preferred_element_type=F32)


def _dot_tn(a, b):
    return lax.dot_general(a.astype(BF16), b.astype(BF16), (((0,), (0,)), ((), ())),
                           preferred_element_type=F32)


def _silu(t):
    return t * jax.nn.sigmoid(t)


def _kv_kernel(mem_ref, w_ref, kbd_ref, vo_ref):
    kv = jnp.dot(mem_ref[0].astype(BF16), w_ref[...], preferred_element_type=F32)
    kt = kv[:, :XATTN_WIDTH].T * XATTN_SCALE
    v = kv[:, XATTN_WIDTH:]
    row_head = lax.broadcasted_iota(jnp.int32, (XATTN_WIDTH, N_MEM), 0) // XATTN_HEAD_DIM
    lane_head = lax.broadcasted_iota(jnp.int32, (N_MEM, XATTN_WIDTH), 1) // XATTN_HEAD_DIM
    for h in range(XATTN_HEADS):
        kbd_ref[0, :, h * N_MEM:(h + 1) * N_MEM] = jnp.where(row_head == h, kt, 0.0).astype(BF16)
        vo_ref[0, h * N_MEM:(h + 1) * N_MEM, :XATTN_WIDTH] = (
            jnp.where(lane_head == h, v, 0.0).astype(BF16))
        vo_ref[0, h * N_MEM:(h + 1) * N_MEM, XATTN_WIDTH:] = (
            jnp.where(lane_head == h, 1.0, 0.0).astype(BF16))


def _memory_operands(mem, mem_kv_w):
    batch = mem.shape[0]
    return pl.pallas_call(
        _kv_kernel,
        grid=(batch,),
        in_specs=[pl.BlockSpec((1, N_MEM, D_MODEL), lambda b: (b, 0, 0)),
                  pl.BlockSpec((D_MODEL, 2 * XATTN_WIDTH), lambda b: (0, 0))],
        out_specs=[pl.BlockSpec((1, XATTN_WIDTH, XATTN_HEADS * N_MEM), lambda b: (b, 0, 0)),
                   pl.BlockSpec((1, XATTN_HEADS * N_MEM, 2 * XATTN_WIDTH), lambda b: (b, 0, 0))],
        out_shape=[jax.ShapeDtypeStruct((batch, XATTN_WIDTH, XATTN_HEADS * N_MEM), BF16),
                   jax.ShapeDtypeStruct((batch, XATTN_HEADS * N_MEM, 2 * XATTN_WIDTH), BF16)],
        name="memory_kv",
    )(mem, mem_kv_w.astype(BF16))


def _cross_attention(qx, kbd_ref, vo_ref):
    s = jnp.dot(qx.astype(BF16), kbd_ref[0], preferred_element_type=F32)
    probs = []
    for h in range(XATTN_HEADS):
        sh = s[:, h * N_MEM:(h + 1) * N_MEM]
        probs.append(jnp.exp(sh - jnp.max(sh, axis=-1, keepdims=True)).astype(BF16))
    nd = jnp.dot(jnp.concatenate(probs, axis=-1), vo_ref[0], preferred_element_type=F32)
    return nd[:, :XATTN_WIDTH] / nd[:, XATTN_WIDTH:]


def _layer_tail(x_ref, tok_ref, proj_ref, qx_start, kbd_ref, vo_ref, wout_ref, lng_ref, lnb_ref, o_ref):
    gate_start = qx_start + XATTN_WIDTH
    for r0 in range(0, x_ref.shape[1], TAIL_ROWS):
        r = slice(r0, r0 + TAIL_ROWS)
        xo = _cross_attention(proj_ref[r, qx_start:gate_start], kbd_ref, vo_ref)
        mixed = jnp.concatenate([tok_ref[r, :], xo], axis=-1) * _silu(proj_ref[r, gate_start:gate_start + MIX_WIDTH])
        y = jnp.dot(mixed.astype(BF16), wout_ref[...], preferred_element_type=F32)
        z = ALPHA * x_ref[0, r, :] + y
        zc = z - jnp.mean(z, axis=-1, keepdims=True)
        var = jnp.mean(zc * zc, axis=-1, keepdims=True)
        o_ref[0, r, :] = zc * lax.rsqrt(var + LN_EPS) * lng_ref[...] + lnb_ref[...]


class _Projector:
    def __init__(self, x_ref, win_ref, proj_ref, xb_ref):
        xb_ref[...] = x_ref[0].astype(BF16)
        self._refs = (win_ref, proj_ref, xb_ref)
        self._starts = list(range(0, proj_ref.shape[1], PROJ_PIECE))

    def _issue(self, c):
        win_ref, proj_ref, xb_ref = self._refs
        self._starts.remove(c)
        w = min(PROJ_PIECE, proj_ref.shape[1] - c)
        proj_ref[:, c:c + w] = jnp.dot(xb_ref[...], win_ref[:, c:c + w], preferred_element_type=F32)

    def step(self, count=1):
        for c in self._starts[:count]:
            self._issue(c)

    def need(self, start, width):
        for c in [c for c in self._starts if c < start + width and c + PROJ_PIECE > start]:
            self._issue(c)

    def through(self, column):
        self.need(0, column)

    def rest(self):
        self.step(len(self._starts))


def _columns(proj, start, width):
    return proj[:, start:start + width]


def _gmlp_layer_kernel(x_ref, win_ref, ws_ref, bs_ref, kbd_ref, vo_ref, wout_ref, lng_ref, lnb_ref, o_ref,
                       proj, xb_ref, tok_ref):
    pieces = _Projector(x_ref, win_ref, proj, xb_ref)
    pieces.through(TOK_WIDTH)
    u = jax.nn.gelu(_columns(proj, 0, TOK_WIDTH))
    pieces.through(2 * TOK_WIDTH)
    v = jax.nn.gelu(_columns(proj, TOK_WIDTH, TOK_WIDTH))
    rows = lax.broadcasted_iota(jnp.int32, (CHUNK, CHUNK), 0)
    cols = lax.broadcasted_iota(jnp.int32, (CHUNK, CHUNK), 1)
    causal = cols <= rows
    for g in range(N_HEADS):
        lanes = slice(g * HEAD_DIM, (g + 1) * HEAD_DIM)
        vg = v[:, lanes]
        vc = vg - jnp.mean(vg, axis=-1, keepdims=True)
        vn = (vc * lax.rsqrt(jnp.mean(vc * vc, axis=-1, keepdims=True) + LN_EPS)).astype(BF16)
        w = jnp.where(causal, ws_ref[g], 0.0).astype(BF16)
        bias = bs_ref[:, g:g + 1]
        pieces.step()
        for c in range(x_ref.shape[1] // CHUNK):
            r = slice(c * CHUNK, (c + 1) * CHUNK)
            spatial = jnp.dot(w, vn[r], preferred_element_type=F32) + bias
            tok_ref[r, lanes] = u[r, lanes] * spatial
    pieces.rest()
    _layer_tail(x_ref, tok_ref, proj, 2 * TOK_WIDTH, kbd_ref, vo_ref, wout_ref, lng_ref, lnb_ref, o_ref)


def _block_diag2(t):
    zero = jnp.zeros_like(t[:, :HEAD_DIM])
    return jnp.concatenate([jnp.concatenate([t[:, :HEAD_DIM], zero], axis=1),
                            jnp.concatenate([zero, t[:, HEAD_DIM:]], axis=1)], axis=0)


def _scale_blocks(pick, e_ref, idx, lanes):
    return jnp.concatenate(
        [pick(j)[j * SUB:(j + 1) * SUB, lanes] * e_ref[idx, j:j + 1, lanes] for j in range(N_SUB)],
        axis=0).astype(BF16)


def _hgrn_layer_kernel(layer, x_ref, win_ref, lbl_ref, ng_ref, kbd_ref, vo_ref, wout_ref, lng_ref, lnb_ref,
                       o_ref, proj, xb_ref, tok_ref, state_ref, gl_ref, s_ref, e_ref):
    @pl.when(pl.program_id(1) == 0)
    def _():
        state_ref[...] = jnp.zeros_like(state_ref)

    pieces = _Projector(x_ref, win_ref, proj, xb_ref)
    pieces.through(TOK_WIDTH)
    q = _silu(_columns(proj, 0, TOK_WIDTH))
    pieces.through(2 * TOK_WIDTH)
    f_logit = _columns(proj, TOK_WIDTH, TOK_WIDTH)

    lbl = lbl_ref[...]
    lbe = jnp.exp(lbl - jnp.max(lbl, axis=0, keepdims=True))
    lbp = lbe / jnp.sum(lbe, axis=0, keepdims=True)
    lb = jnp.sum(lbp[:layer + 1], axis=0, keepdims=True) - lbp[0:1]
    f = lb + (1.0 - lb) * jax.nn.sigmoid(f_logit)
    pieces.through(3 * TOK_WIDTH)
    log_f = jnp.log(f)
    k = 1.0 - f
    val = _columns(proj, 2 * TOK_WIDTH, TOK_WIDTH)

    tt = lax.broadcasted_iota(jnp.int32, (CHUNK, PAIR), 0)
    ss = lax.broadcasted_iota(jnp.int32, (CHUNK, PAIR), 1) % CHUNK
    level_masks = [((tt // (2 * h)) == (ss // (2 * h))) & ((tt % (2 * h)) >= h) & ((ss % (2 * h)) < h)
                   for h in SPLIT_HALVES]
    diag_mask = ((tt // SUB) == (ss // SUB)) & (ss <= tt)
    br = lax.broadcasted_iota(jnp.int32, (CHUNK, CHUNK), 0)
    bc = lax.broadcasted_iota(jnp.int32, (CHUNK, CHUNK), 1)
    block_tril = jnp.where(((br // SUB) == (bc // SUB)) & (bc <= br), 1.0, 0.0).astype(BF16)
    sub_row = lax.broadcasted_iota(jnp.int32, (N_SUB, 1), 0)
    n_lvl = len(SPLIT_HALVES)

    n_chunks = x_ref.shape[1] // CHUNK
    ql_all, kl_all, decay_all = [], [], []
    for c in range(n_chunks):
        r = slice(c * CHUNK, (c + 1) * CHUNK)
        lfc = log_f[r]
        hi = lfc.astype(BF16)
        lo = (lfc - hi.astype(F32)).astype(BF16)
        gl = (jnp.dot(block_tril, hi, preferred_element_type=F32)
              + jnp.dot(block_tril, lo, preferred_element_type=F32))
        ql_all.append(q[r] * jnp.exp(gl))
        kl_all.append(k[r] * jnp.exp(-gl))

        for h in range(N_HEADS):
            gl_ref[c, h] = gl[:, h * HEAD_DIM:(h + 1) * HEAD_DIM]
        bsum = jnp.concatenate(
            [gl_ref[c, h, pl.ds(SUB - 1, N_SUB, stride=SUB), :] for h in range(N_HEADS)], axis=1)
        incl = bsum
        for d in (1, 2, 4):
            incl = incl + jnp.where(sub_row >= d, pltpu.roll(incl, d, 0), 0.0)
        s = incl - bsum
        s_ref[c] = s
        total = s_ref[c, N_SUB - 1:N_SUB, :] + gl[CHUNK - 1:CHUNK, :]
        for li, half in enumerate(SPLIT_HALVES):
            nb = half // SUB
            boundary = s_ref[c, nb:nb + 1, :]
            for g0 in range(2 * nb, N_SUB, 2 * nb):
                boundary = jnp.where(sub_row >= g0, s_ref[c, g0 + nb:g0 + nb + 1, :], boundary)
            e_ref[c, li] = jnp.exp(-jnp.abs(s - boundary))
        e_ref[c, n_lvl] = jnp.exp(s)
        e_ref[c, n_lvl + 1] = jnp.exp(total - s)
        decay_all.append(jnp.exp(total))
        pieces.step()

    pair_lanes = [slice(p * PAIR, (p + 1) * PAIR) for p in range(N_HEADS // 2)]
    zero = jnp.zeros((HEAD_DIM, HEAD_DIM), BF16)
    for c in range(n_chunks):
        r = slice(c * CHUNK, (c + 1) * CHUNK)
        ql, kl, e_c = ql_all[c], kl_all[c], e_ref.at[c]
        scores = [jnp.where(diag_mask, _dot_nt(ql[:, lanes], _block_diag2(kl[:, lanes].astype(BF16))), 0.0)
                  for lanes in pair_lanes]
        for li, half in enumerate(SPLIT_HALVES):
            nb = half // SUB
            for p, lanes in enumerate(pair_lanes):
                xs = _scale_blocks(lambda j: ql if (j // nb) % 2 else kl, e_c, li, lanes)
                scores[p] = jnp.where(level_masks[li], _dot_nt(xs, _block_diag2(xs)), scores[p])
        outs, k_ends, vcs = [], [], []
        for p, lanes in enumerate(pair_lanes):
            q_dec = _scale_blocks(lambda j: ql, e_c, n_lvl, lanes)
            k_ends.append(_scale_blocks(lambda j: kl, e_c, n_lvl + 1, lanes))
            vcs.append(val[r, lanes].astype(BF16))
            state_bd = jnp.concatenate(
                [jnp.concatenate([state_ref[2 * p].astype(BF16), zero], axis=1),
                 jnp.concatenate([zero, state_ref[2 * p + 1].astype(BF16)], axis=1)], axis=0)
            outs.append(_dot(scores[p], _block_diag2(vcs[p])) + _dot_nt(q_dec, state_bd))
        for p in range(N_HEADS // 2):
            for i in range(2):
                h = 2 * p + i
                hl = slice(i * HEAD_DIM, (i + 1) * HEAD_DIM)
                head_lanes = slice(h * HEAD_DIM, (h + 1) * HEAD_DIM)
                state_ref[h] = (state_ref[h] * decay_all[c][:, head_lanes]
                                + _dot_tn(vcs[p][:, hl], k_ends[p][:, hl]))
                oh = outs[p][:, hl]
                oh = oh * lax.rsqrt(jnp.mean(oh * oh, axis=-1, keepdims=True) + RMS_EPS)
                tok_ref[r, head_lanes] = oh * ng_ref[:, head_lanes]
        pieces.step()
    pieces.rest()
    _layer_tail(x_ref, tok_ref, proj, 3 * TOK_WIDTH, kbd_ref, vo_ref, wout_ref, lng_ref, lnb_ref, o_ref)


def _pool_layer_kernel(x_ref, win_ref, wpool_ref, scale_ref, kbd_ref, vo_ref, wout_ref, lng_ref, lnb_ref, o_ref,
                       proj, xb_ref, tok_ref, pbuf_ref):
    si = pl.program_id(1)
    tile = x_ref.shape[1]

    @pl.when(si == 0)
    def _():
        pbuf_ref[:POOL_HALO] = jnp.zeros((POOL_HALO, TOK_WIDTH), F32)

    pieces = _Projector(x_ref, win_ref, proj, xb_ref)
    pos = si * tile + lax.broadcasted_iota(jnp.int32, (tile, 1), 0)
    inv_count = [1.0 / jnp.minimum(pos + 1, w).astype(F32) for w in POOL_WINDOWS]
    diffs = []
    for c0 in range(0, TOK_WIDTH, PAIR):
        cols = slice(c0, c0 + PAIR)
        groups = range(c0 // POOL_GROUP, (c0 + PAIR - 1) // POOL_GROUP + 1)
        pieces.need(c0, PAIR)
        p = _columns(proj, c0, PAIR)
        pbuf_ref[POOL_HALO:, cols] = p
        acc = pbuf_ref[:, cols]
        pieces.step()
        chan = c0 + lax.broadcasted_iota(jnp.int32, (1, PAIR), 1)
        pooled = None
        for gi in range(groups[-1] + 1):
            acc = acc + pltpu.roll(acc, POOL_WINDOWS[gi] // 2, 0)
            if gi in groups:
                mean = acc[POOL_HALO:] * inv_count[gi]
                pooled = mean if pooled is None else jnp.where(chan >= gi * POOL_GROUP, mean, pooled)
        pbuf_ref[:POOL_HALO, cols] = p[tile - POOL_HALO:]
        diffs.append((pooled - p).astype(BF16))
    pieces.rest()
    diff = jnp.concatenate(diffs, axis=1)
    for c0 in range(0, TOK_WIDTH, PAIR):
        k0 = (c0 // POOL_GROUP) * POOL_GROUP // HEAD_DIM * HEAD_DIM
        k1 = -(-((c0 + PAIR - 1) // POOL_GROUP + 1) * POOL_GROUP // HEAD_DIM) * HEAD_DIM
        tok_ref[:, c0:c0 + PAIR] = (
            jnp.dot(diff[:, k0:k1], wpool_ref[k0:k1, c0:c0 + PAIR], preferred_element_type=F32)
            * scale_ref[:, c0:c0 + PAIR])
    _layer_tail(x_ref, tok_ref, proj, TOK_WIDTH, kbd_ref, vo_ref, wout_ref, lng_ref, lnb_ref, o_ref)


def _lru_layer_kernel(x_ref, win_ref, convw_ref, convb_ref, wg_ref, bg_ref, ap_ref, kbd_ref, vo_ref, wout_ref,
                      lng_ref, lnb_ref, o_ref, proj, xb_ref, tok_ref, cbuf_ref, a_ref, b_ref, carry_ref,
                      segp_ref, segh_ref):
    si = pl.program_id(1)
    tile = x_ref.shape[1]

    @pl.when(si == 0)
    def _():
        cbuf_ref[:CONV_HALO] = jnp.zeros((CONV_HALO, TOK_WIDTH), F32)
        carry_ref[...] = jnp.zeros_like(carry_ref)

    pieces = _Projector(x_ref, win_ref, proj, xb_ref)
    seg = tile // SEGMENTS
    seg_stride = seg + SEGMENT_PAD
    ap = ap_ref[...]
    neg_softplus = -(jnp.maximum(-ap, 0.0) + jnp.log1p(jnp.exp(-jnp.abs(ap))))
    first = (si * tile + lax.broadcasted_iota(jnp.int32, (tile, 1), 0)) == 0
    for h in range(N_HEADS):
        lanes = slice(h * HEAD_DIM, (h + 1) * HEAD_DIM)
        pieces.need(h * HEAD_DIM, HEAD_DIM)
        xin = _columns(proj, h * HEAD_DIM, HEAD_DIM)
        cbuf_ref[CONV_HALO:, lanes] = xin
        xh = convw_ref[CONV_WIDTH - 1:CONV_WIDTH, lanes] * xin + convb_ref[:, lanes]
        for j in range(1, CONV_WIDTH):
            xh = xh + (convw_ref[CONV_WIDTH - 1 - j:CONV_WIDTH - j, lanes]
                       * cbuf_ref[CONV_HALO - j:CONV_HALO - j + tile, lanes])
        cbuf_ref[:CONV_HALO, lanes] = xin[tile - CONV_HALO:]
        pieces.step()
        gates = jax.nn.sigmoid(_dot(xh, wg_ref[h]) + bg_ref[h])
        log_a = LRU_C * gates[:, HEAD_DIM:] * neg_softplus[:, lanes]
        th = jnp.tanh(-log_a)
        sq = 2.0 * th / (1.0 + th)
        mult = jnp.where(first, 1.0, jnp.where(sq > 0.0, sq * lax.rsqrt(sq), 0.0))
        a_val = jnp.exp(log_a)
        b_val = mult * gates[:, :HEAD_DIM] * xh
        for s in range(SEGMENTS):
            a_ref[h, s * seg_stride:s * seg_stride + seg] = a_val[s * seg:(s + 1) * seg]
            b_ref[h, s * seg_stride:s * seg_stride + seg] = b_val[s * seg:(s + 1) * seg]

    rows = [pl.ds(r, SEGMENTS, stride=seg_stride) for r in range(seg)]
    for h in range(N_HEADS):
        acc = jnp.zeros((SEGMENTS, HEAD_DIM), F32)
        prod = jnp.ones((SEGMENTS, HEAD_DIM), F32)
        for r in rows:
            ar = a_ref[h, r, :]
            acc = ar * acc + b_ref[h, r, :]
            prod = prod * ar
        segp_ref[h] = prod
        segh_ref[h] = acc
        for s in range(1, SEGMENTS):
            carry_ref[h, s:s + 1] = (segp_ref[h, s - 1:s] * carry_ref[h, s - 1:s] + segh_ref[h, s - 1:s])
        acc = carry_ref[h]
        for r in rows:
            acc = a_ref[h, r, :] * acc + b_ref[h, r, :]
            b_ref[h, r, :] = acc
        last = (SEGMENTS - 1) * seg_stride + seg - 1
        carry_ref[h, 0:1] = b_ref[h, last:last + 1]
        for s in range(SEGMENTS):
            tok_ref[s * seg:(s + 1) * seg, h * HEAD_DIM:(h + 1) * HEAD_DIM] = (
                b_ref[h, s * seg_stride:s * seg_stride + seg])
    pieces.rest()
    _layer_tail(x_ref, tok_ref, proj, TOK_WIDTH, kbd_ref, vo_ref, wout_ref, lng_ref, lnb_ref, o_ref)


def _full(arr):
    nd = arr.ndim
    return pl.BlockSpec(arr.shape, lambda b, s: (0,) * nd, pipeline_mode=pl.Buffered(1))


def _run_layer(body, x, params, kbd, vo, w_out, ln_g, ln_b, scratch, name):
    batch, seq, _ = x.shape
    tile = SEQ_TILE
    x_spec = pl.BlockSpec((1, tile, D_MODEL), lambda b, s: (b, s, 0))
    shared = [w_out.astype(BF16), ln_g.reshape(1, D_MODEL), ln_b.reshape(1, D_MODEL)]
    in_specs = ([x_spec] + [_full(p) for p in params]
                + [pl.BlockSpec((1,) + kbd.shape[1:], lambda b, s: (b, 0, 0)),
                   pl.BlockSpec((1,) + vo.shape[1:], lambda b, s: (b, 0, 0))]
                + [_full(p) for p in shared])
    common = [pltpu.VMEM((tile, params[0].shape[1]), F32),
              pltpu.VMEM((tile, D_MODEL), BF16),
              pltpu.VMEM((tile, TOK_WIDTH), F32)]
    return pl.pallas_call(
        body,
        grid=(batch, seq // tile),
        in_specs=in_specs,
        out_specs=x_spec,
        out_shape=jax.ShapeDtypeStruct(x.shape, x.dtype),
        scratch_shapes=common + scratch,
        compiler_params=pltpu.CompilerParams(
            dimension_semantics=("arbitrary", "arbitrary"), vmem_limit_bytes=VMEM_LIMIT_BYTES),
        name=name,
    )(x, *params, kbd, vo, *shared)


def kernel(x, mem, mem_kv_w, ln_g, ln_b, w_out, hgrn_lb_logits, a_w_in, a_w_s, a_b_s, b_w_in, b_norm_g,
           c_w_in, c_w_pool, c_scale, d_w_in, d_conv_w, d_conv_b, d_w_gx, d_b_gx, d_w_ga, d_b_ga,
           d_a_param):
    tile = SEQ_TILE
    assert x.shape[1] % tile == 0 and tile % CHUNK == 0
    kbd, vo = _memory_operands(mem, mem_kv_w)
    for i in range(DEPTH):
        kind, j = i % 4, i // 4
        if kind == 0:
            params = [a_w_in[j].astype(BF16), a_w_s[j], a_b_s[j].T]
            body, scratch = _gmlp_layer_kernel, []
        elif kind == 1:
            params = [b_w_in[j].astype(BF16), hgrn_lb_logits, b_norm_g[j].reshape(1, TOK_WIDTH)]
            body = functools.partial(_hgrn_layer_kernel, i)
            scratch = [pltpu.VMEM((N_HEADS, HEAD_DIM, HEAD_DIM), F32),
                       pltpu.VMEM((tile // CHUNK, N_HEADS, CHUNK, HEAD_DIM), F32),
                       pltpu.VMEM((tile // CHUNK, N_SUB, TOK_WIDTH), F32),
                       pltpu.VMEM((tile // CHUNK, len(SPLIT_HALVES) + 2, N_SUB, TOK_WIDTH), F32)]
        elif kind == 2:
            wpool = jax.scipy.linalg.block_diag(*[c_w_pool[j, g] for g in range(len(POOL_WINDOWS))])
            params = [c_w_in[j].astype(BF16), wpool.astype(BF16), c_scale[j].reshape(1, TOK_WIDTH)]
            body = _pool_layer_kernel
            scratch = [pltpu.VMEM((POOL_HALO + tile, TOK_WIDTH), F32)]
        else:
            wg = jnp.concatenate([d_w_gx[j], d_w_ga[j]], axis=-1).astype(BF16)
            bg = jnp.concatenate([d_b_gx[j], d_b_ga[j]], axis=-1)[:, None, :]
            params = [d_w_in[j].astype(BF16), d_conv_w[j], d_conv_b[j].reshape(1, TOK_WIDTH), wg, bg,
                      d_a_param[j].reshape(1, TOK_WIDTH)]
            body = _lru_layer_kernel
            scratch = [pltpu.VMEM((CONV_HALO + tile, TOK_WIDTH), F32),
                       pltpu.VMEM((N_HEADS, tile + SEGMENTS * SEGMENT_PAD, HEAD_DIM), F32),
                       pltpu.VMEM((N_HEADS, tile + SEGMENTS * SEGMENT_PAD, HEAD_DIM), F32),
                       pltpu.VMEM((N_HEADS, SEGMENTS, HEAD_DIM), F32),
                       pltpu.VMEM((N_HEADS, SEGMENTS, HEAD_DIM), F32),
                       pltpu.VMEM((N_HEADS, SEGMENTS, HEAD_DIM), F32)]
        x = _run_layer(body, x, params, kbd, vo, w_out[i], ln_g[i], ln_b[i], scratch,
                       name=f"layer{i}")
    return x
```

```python
import functools
import math

import jax
import jax.numpy as jnp
from jax import lax
from jax.experimental import pallas as pl
from jax.experimental.pallas import tpu as pltpu

F32 = jnp.float32
BF16 = jnp.bfloat16

D_MODEL = 1024
DEPTH = 4
N_MEM = 256
MIX_WIDTH = D_MODEL
XATTN_HEADS = 4
XATTN_HEAD_DIM = 64
XATTN_WIDTH = XATTN_HEADS * XATTN_HEAD_DIM
XATTN_SCALE = XATTN_HEAD_DIM ** -0.5
assert math.frexp(XATTN_SCALE)[0] == 0.5
TOK_WIDTH = MIX_WIDTH - XATTN_WIDTH
HEAD_DIM = 128
N_HEADS = TOK_WIDTH // HEAD_DIM
CHUNK = 128
SUB = 16
N_SUB = CHUNK // SUB
SPLIT_HALVES = (CHUNK // 2, CHUNK // 4, CHUNK // 8)
PAIR = 2 * HEAD_DIM
SEGMENTS = 8
SEGMENT_PAD = 8
POOL_WINDOWS = (2, 4, 8, 16)
POOL_GROUP = TOK_WIDTH // len(POOL_WINDOWS)
POOL_HALO = 16
CONV_WIDTH = 4
CONV_HALO = 8
LRU_C = 8.0
ALPHA = (2 * DEPTH) ** 0.25
LN_EPS = 1e-5
RMS_EPS = 1e-6

SEQ_TILE = 512
PROJ_PIECE = 256
TAIL_ROWS = 256
VMEM_LIMIT_BYTES = 56 * 1024 * 1024


def _dot(a, b):
    return jnp.dot(a.astype(BF16), b.astype(BF16), preferred_element_type=F32)


def _dot_nt(a, b):
    return lax.dot_general(a.astype(BF16), b.astype(BF16), (((1,), (1,)), ((), ())),
                           preferred_element_type=F32)


def _dot_tn(a, b):
    return lax.dot_general(a.astype(BF16), b.astype(BF16), (((0,), (0,)), ((), ())),
                           preferred_element_type=F32)


def _silu(t):
    return t * jax.nn.sigmoid(t)


def _kv_kernel(mem_ref, w_ref, kbd_ref, vo_ref):
    kv = jnp.dot(mem_ref[0].astype(BF16), w_ref[...], preferred_element_type=F32)
    kt = kv[:, :XATTN_WIDTH].T * XATTN_SCALE
    v = kv[:, XATTN_WIDTH:]
    row_head = lax.broadcasted_iota(jnp.int32, (XATTN_WIDTH, N_MEM), 0) // XATTN_HEAD_DIM
    lane_head = lax.broadcasted_iota(jnp.int32, (N_MEM, XATTN_WIDTH), 1) // XATTN_HEAD_DIM
    for h in range(XATTN_HEADS):
        kbd_ref[0, :, h * N_MEM:(h + 1) * N_MEM] = jnp.where(row_head == h, kt, 0.0).astype(BF16)
        vo_ref[0, h * N_MEM:(h + 1) * N_MEM, :XATTN_WIDTH] = (
            jnp.where(lane_head == h, v, 0.0).astype(BF16))
        vo_ref[0, h * N_MEM:(h + 1) * N_MEM, XATTN_WIDTH:] = (
            jnp.where(lane_head == h, 1.0, 0.0).astype(BF16))


def _memory_operands(mem, mem_kv_w):
    batch = mem.shape[0]
    return pl.pallas_call(
        _kv_kernel,
        grid=(batch,),
        in_specs=[pl.BlockSpec((1, N_MEM, D_MODEL), lambda b: (b, 0, 0)),
                  pl.BlockSpec((D_MODEL, 2 * XATTN_WIDTH), lambda b: (0, 0))],
        out_specs=[pl.BlockSpec((1, XATTN_WIDTH, XATTN_HEADS * N_MEM), lambda b: (b, 0, 0)),
                   pl.BlockSpec((1, XATTN_HEADS * N_MEM, 2 * XATTN_WIDTH), lambda b: (b, 0, 0))],
        out_shape=[jax.ShapeDtypeStruct((batch, XATTN_WIDTH, XATTN_HEADS * N_MEM), BF16),
                   jax.ShapeDtypeStruct((batch, XATTN_HEADS * N_MEM, 2 * XATTN_WIDTH), BF16)],
        name="memory_kv",
    )(mem, mem_kv_w.astype(BF16))


def _cross_attention(qx, kbd_ref, vo_ref):
    s = jnp.dot(qx.astype(BF16), kbd_ref[0], preferred_element_type=F32)
    probs = []
    for h in range(XATTN_HEADS):
        sh = s[:, h * N_MEM:(h + 1) * N_MEM]
        probs.append(jnp.exp(sh - jnp.max(sh, axis=-1, keepdims=True)).astype(BF16))
    nd = jnp.dot(jnp.concatenate(probs, axis=-1), vo_ref[0], preferred_element_type=F32)
    return nd[:, :XATTN_WIDTH] / nd[:, XATTN_WIDTH:]


def _layer_tail(x_ref, mix_ref, proj_ref, gate_start, wout_ref, lng_ref, lnb_ref, o_ref):
    for r0 in range(0, x_ref.shape[1], TAIL_ROWS):
        r = slice(r0, r0 + TAIL_ROWS)
        mixed = mix_ref[r, :] * _silu(proj_ref[r, gate_start:gate_start + MIX_WIDTH])
        y = jnp.dot(mixed.astype(BF16), wout_ref[...], preferred_element_type=F32)
        z = ALPHA * x_ref[0, r, :] + y
        zc = z - jnp.mean(z, axis=-1, keepdims=True)
        var = jnp.mean(zc * zc, axis=-1, keepdims=True)
        o_ref[0, r, :] = zc * lax.rsqrt(var + LN_EPS) * lng_ref[...] + lnb_ref[...]


class _Projector:
    def __init__(self, x_ref, win_ref, proj_ref, xb_ref):
        xb_ref[...] = x_ref[0].astype(BF16)
        self._refs = (win_ref, proj_ref, xb_ref)
        self._starts = list(range(0, proj_ref.shape[1], PROJ_PIECE))

    def _issue(self, c):
        win_ref, proj_ref, xb_ref = self._refs
        self._starts.remove(c)
        w = min(PROJ_PIECE, proj_ref.shape[1] - c)
        proj_ref[:, c:c + w] = jnp.dot(xb_ref[...], win_ref[:, c:c + w], preferred_element_type=F32)

    def step(self, count=1):
        for c in self._starts[:count]:
            self._issue(c)

    def need(self, start, width):
        for c in [c for c in self._starts if c < start + width and c + PROJ_PIECE > start]:
            self._issue(c)

    def through(self, column):
        self.need(0, column)

    def rest(self):
        self.step(len(self._starts))


class _Attender:
    def __init__(self, pieces, proj_ref, qx_start, kbd_ref, vo_ref, mix_ref):
        self._args = (pieces, proj_ref, qx_start, kbd_ref, vo_ref, mix_ref)
        self._blocks = list(range(0, mix_ref.shape[0], TAIL_ROWS))

    def step(self, count=1):
        pieces, proj_ref, qx_start, kbd_ref, vo_ref, mix_ref = self._args
        pieces.need(qx_start, XATTN_WIDTH)
        for r0 in self._blocks[:count]:
            self._blocks.remove(r0)
            r = slice(r0, r0 + TAIL_ROWS)
            mix_ref[r, TOK_WIDTH:] = _cross_attention(proj_ref[r, qx_start:qx_start + XATTN_WIDTH], kbd_ref, vo_ref)

    def rest(self):
        self.step(len(self._blocks))


def _columns(proj, start, width):
    return proj[:, start:start + width]


def _gmlp_layer_kernel(x_ref, win_ref, ws_ref, bs_ref, kbd_ref, vo_ref, wout_ref, lng_ref, lnb_ref, o_ref,
                       proj, xb_ref, tok_ref):
    pieces = _Projector(x_ref, win_ref, proj, xb_ref)
    pieces.through(TOK_WIDTH)
    u = jax.nn.gelu(_columns(proj, 0, TOK_WIDTH))
    pieces.through(2 * TOK_WIDTH)
    v = jax.nn.gelu(_columns(proj, TOK_WIDTH, TOK_WIDTH))
    rows = lax.broadcasted_iota(jnp.int32, (CHUNK, CHUNK), 0)
    cols = lax.broadcasted_iota(jnp.int32, (CHUNK, CHUNK), 1)
    causal = cols <= rows
    for g in range(N_HEADS):
        lanes = slice(g * HEAD_DIM, (g + 1) * HEAD_DIM)
        vg = v[:, lanes]
        vc = vg - jnp.mean(vg, axis=-1, keepdims=True)
        vn = (vc * lax.rsqrt(jnp.mean(vc * vc, axis=-1, keepdims=True) + LN_EPS)).astype(BF16)
        w = jnp.where(causal, ws_ref[g], 0.0).astype(BF16)
        bias = bs_ref[:, g:g + 1]
        pieces.step()
        for c in range(x_ref.shape[1] // CHUNK):
            r = slice(c * CHUNK, (c + 1) * CHUNK)
            spatial = jnp.dot(w, vn[r], preferred_element_type=F32) + bias
            tok_ref[r, lanes] = u[r, lanes] * spatial
    _Attender(pieces, proj, 2 * TOK_WIDTH, kbd_ref, vo_ref, tok_ref).rest()
    pieces.rest()
    _layer_tail(x_ref, tok_ref, proj, 2 * TOK_WIDTH + XATTN_WIDTH, wout_ref, lng_ref, lnb_ref, o_ref)


def _block_diag2(t):
    zero = jnp.zeros_like(t[:, :HEAD_DIM])
    return jnp.concatenate([jnp.concatenate([t[:, :HEAD_DIM], zero], axis=1),
                            jnp.concatenate([zero, t[:, HEAD_DIM:]], axis=1)], axis=0)


def _scale_blocks(pick, e_ref, idx, lanes):
    return jnp.concatenate(
        [pick(j)[j * SUB:(j + 1) * SUB, lanes] * e_ref[idx, j:j + 1, lanes] for j in range(N_SUB)],
        axis=0).astype(BF16)


def _hgrn_layer_kernel(layer, x_ref, win_ref, lbl_ref, ng_ref, kbd_ref, vo_ref, wout_ref, lng_ref, lnb_ref,
                       o_ref, proj, xb_ref, tok_ref, state_ref, gl_ref, s_ref, e_ref):
    @pl.when(pl.program_id(1) == 0)
    def _():
        state_ref[...] = jnp.zeros_like(state_ref)

    pieces = _Projector(x_ref, win_ref, proj, xb_ref)
    pieces.through(TOK_WIDTH)
    q = _silu(_columns(proj, 0, TOK_WIDTH))
    pieces.through(2 * TOK_WIDTH)
    f_logit = _columns(proj, TOK_WIDTH, TOK_WIDTH)

    lbl = lbl_ref[...]
    lbe = jnp.exp(lbl - jnp.max(lbl, axis=0, keepdims=True))
    lbp = lbe / jnp.sum(lbe, axis=0, keepdims=True)
    lb = jnp.sum(lbp[:layer + 1], axis=0, keepdims=True) - lbp[0:1]
    f = lb + (1.0 - lb) * jax.nn.sigmoid(f_logit)
    pieces.through(3 * TOK_WIDTH)
    log_f = jnp.log(f)
    k = 1.0 - f
    val = _columns(proj, 2 * TOK_WIDTH, TOK_WIDTH)

    tt = lax.broadcasted_iota(jnp.int32, (CHUNK, PAIR), 0)
    ss = lax.broadcasted_iota(jnp.int32, (CHUNK, PAIR), 1) % CHUNK
    level_masks = [((tt // (2 * h)) == (ss // (2 * h))) & ((tt % (2 * h)) >= h) & ((ss % (2 * h)) < h)
                   for h in SPLIT_HALVES]
    diag_mask = ((tt // SUB) == (ss // SUB)) & (ss <= tt)
    br = lax.broadcasted_iota(jnp.int32, (CHUNK, CHUNK), 0)
    bc = lax.broadcasted_iota(jnp.int32, (CHUNK, CHUNK), 1)
    block_tril = jnp.where(((br // SUB) == (bc // SUB)) & (bc <= br), 1.0, 0.0).astype(BF16)
    sub_row = lax.broadcasted_iota(jnp.int32, (N_SUB, 1), 0)
    n_lvl = len(SPLIT_HALVES)

    n_chunks = x_ref.shape[1] // CHUNK
    ql_all, kl_all, decay_all = [], [], []
    for c in range(n_chunks):
        r = slice(c * CHUNK, (c + 1) * CHUNK)
        lfc = log_f[r]
        hi = lfc.astype(BF16)
        lo = (lfc - hi.astype(F32)).astype(BF16)
        gl = (jnp.dot(block_tril, hi, preferred_element_type=F32)
              + jnp.dot(block_tril, lo, preferred_element_type=F32))
        ql_all.append(q[r] * jnp.exp(gl))
        kl_all.append(k[r] * jnp.exp(-gl))

        for h in range(N_HEADS):
            gl_ref[c, h] = gl[:, h * HEAD_DIM:(h + 1) * HEAD_DIM]
        bsum = jnp.concatenate(
            [gl_ref[c, h, pl.ds(SUB - 1, N_SUB, stride=SUB), :] for h in range(N_HEADS)], axis=1)
        incl = bsum
        for d in (1, 2, 4):
            incl = incl + jnp.where(sub_row >= d, pltpu.roll(incl, d, 0), 0.0)
        s = incl - bsum
        s_ref[c] = s
        total = s_ref[c, N_SUB - 1:N_SUB, :] + gl[CHUNK - 1:CHUNK, :]
        for li, half in enumerate(SPLIT_HALVES):
            nb = half // SUB
            boundary = s_ref[c, nb:nb + 1, :]
            for g0 in range(2 * nb, N_SUB, 2 * nb):
                boundary = jnp.where(sub_row >= g0, s_ref[c, g0 + nb:g0 + nb + 1, :], boundary)
            e_ref[c, li] = jnp.exp(-jnp.abs(s - boundary))
        e_ref[c, n_lvl] = jnp.exp(s)
        e_ref[c, n_lvl + 1] = jnp.exp(total - s)
        decay_all.append(jnp.exp(total))
        pieces.step()

    pair_lanes = [slice(p * PAIR, (p + 1) * PAIR) for p in range(N_HEADS // 2)]
    zero = jnp.zeros((HEAD_DIM, HEAD_DIM), BF16)
    attend = _Attender(pieces, proj, 3 * TOK_WIDTH, kbd_ref, vo_ref, tok_ref)
    for c in range(n_chunks):
        r = slice(c * CHUNK, (c + 1) * CHUNK)
        ql, kl, e_c = ql_all[c], kl_all[c], e_ref.at[c]
        scores = [jnp.where(diag_mask, _dot_nt(ql[:, lanes], _block_diag2(kl[:, lanes].astype(BF16))), 0.0)
                  for lanes in pair_lanes]
        for li, half in enumerate(SPLIT_HALVES):
            nb = half // SUB
            for p, lanes in enumerate(pair_lanes):
                xs = _scale_blocks(lambda j: ql if (j // nb) % 2 else kl, e_c, li, lanes)
                scores[p] = jnp.where(level_masks[li], _dot_nt(xs, _block_diag2(xs)), scores[p])
        outs, k_ends, vcs = [], [], []
        for p, lanes in enumerate(pair_lanes):
            q_dec = _scale_blocks(lambda j: ql, e_c, n_lvl, lanes)
            k_ends.append(_scale_blocks(lambda j: kl, e_c, n_lvl + 1, lanes))
            vcs.append(val[r, lanes].astype(BF16))
            state_bd = jnp.concatenate(
                [jnp.concatenate([state_ref[2 * p].astype(BF16), zero], axis=1),
                 jnp.concatenate([zero, state_ref[2 * p + 1].astype(BF16)], axis=1)], axis=0)
            outs.append(_dot(scores[p], _block_diag2(vcs[p])) + _dot_nt(q_dec, state_bd))
        for p in range(N_HEADS // 2):
            for i in range(2):
                h = 2 * p + i
                hl = slice(i * HEAD_DIM, (i + 1) * HEAD_DIM)
                head_lanes = slice(h * HEAD_DIM, (h + 1) * HEAD_DIM)
                state_ref[h] = (state_ref[h] * decay_all[c][:, head_lanes]
                                + _dot_tn(vcs[p][:, hl], k_ends[p][:, hl]))
                oh = outs[p][:, hl]
                oh = oh * lax.rsqrt(jnp.mean(oh * oh, axis=-1, keepdims=True) + RMS_EPS)
                tok_ref[r, head_lanes] = oh * ng_ref[:, head_lanes]
        pieces.step()
        if c % 2 == 0:
            attend.step()
    attend.rest()
    pieces.rest()
    _layer_tail(x_ref, tok_ref, proj, 3 * TOK_WIDTH + XATTN_WIDTH, wout_ref, lng_ref, lnb_ref, o_ref)


def _pool_layer_kernel(x_ref, win_ref, wpool_ref, scale_ref, kbd_ref, vo_ref, wout_ref, lng_ref, lnb_ref, o_ref,
                       proj, xb_ref, tok_ref, pbuf_ref):
    si = pl.program_id(1)
    tile = x_ref.shape[1]

    @pl.when(si == 0)
    def _():
        pbuf_ref[:POOL_HALO] = jnp.zeros((POOL_HALO, TOK_WIDTH), F32)

    pieces = _Projector(x_ref, win_ref, proj, xb_ref)
    pieces.through(TOK_WIDTH)
    p = _columns(proj, 0, TOK_WIDTH)

    pbuf_ref[POOL_HALO:] = p
    ext = pbuf_ref[...]
    pieces.step(2)
    sums = []
    acc = ext
    for w in POOL_WINDOWS:
        acc = acc + pltpu.roll(acc, w // 2, 0)
        sums.append(acc[POOL_HALO:])
    pbuf_ref[:POOL_HALO] = p[tile - POOL_HALO:]

    pos = si * tile + lax.broadcasted_iota(jnp.int32, (tile, 1), 0)
    chan = lax.broadcasted_iota(jnp.int32, (1, TOK_WIDTH), 1)
    pooled = None
    for gi in reversed(range(len(POOL_WINDOWS))):
        w = POOL_WINDOWS[gi]
        mean = sums[gi] * (1.0 / jnp.minimum(pos + 1, w).astype(F32))
        pooled = mean if pooled is None else jnp.where(chan < (gi + 1) * POOL_GROUP, mean, pooled)
    pieces.rest()
    tok_ref[:, :TOK_WIDTH] = _dot(pooled - p, wpool_ref[...]) * scale_ref[...]
    _Attender(pieces, proj, TOK_WIDTH, kbd_ref, vo_ref, tok_ref).rest()
    _layer_tail(x_ref, tok_ref, proj, TOK_WIDTH + XATTN_WIDTH, wout_ref, lng_ref, lnb_ref, o_ref)


def _lru_layer_kernel(x_ref, win_ref, convw_ref, convb_ref, wg_ref, bg_ref, ap_ref, kbd_ref, vo_ref, wout_ref,
                      lng_ref, lnb_ref, o_ref, proj, xb_ref, tok_ref, cbuf_ref, a_ref, b_ref, carry_ref,
                      segp_ref, segh_ref):
    si = pl.program_id(1)
    tile = x_ref.shape[1]

    @pl.when(si == 0)
    def _():
        cbuf_ref[:CONV_HALO] = jnp.zeros((CONV_HALO, TOK_WIDTH), F32)
        carry_ref[...] = jnp.zeros_like(carry_ref)

    pieces = _Projector(x_ref, win_ref, proj, xb_ref)
    seg = tile // SEGMENTS
    seg_stride = seg + SEGMENT_PAD
    ap = ap_ref[...]
    neg_softplus = -(jnp.maximum(-ap, 0.0) + jnp.log1p(jnp.exp(-jnp.abs(ap))))
    first = (si * tile + lax.broadcasted_iota(jnp.int32, (tile, 1), 0)) == 0
    for h in range(N_HEADS):
        lanes = slice(h * HEAD_DIM, (h + 1) * HEAD_DIM)
        pieces.need(h * HEAD_DIM, HEAD_DIM)
        xin = _columns(proj, h * HEAD_DIM, HEAD_DIM)
        cbuf_ref[CONV_HALO:, lanes] = xin
        xh = convw_ref[CONV_WIDTH - 1:CONV_WIDTH, lanes] * xin + convb_ref[:, lanes]
        for j in range(1, CONV_WIDTH):
            xh = xh + (convw_ref[CONV_WIDTH - 1 - j:CONV_WIDTH - j, lanes]
                       * cbuf_ref[CONV_HALO - j:CONV_HALO - j + tile, lanes])
        cbuf_ref[:CONV_HALO, lanes] = xin[tile - CONV_HALO:]
        pieces.step()
        gates = jax.nn.sigmoid(_dot(xh, wg_ref[h]) + bg_ref[h])
        log_a = LRU_C * gates[:, HEAD_DIM:] * neg_softplus[:, lanes]
        th = jnp.tanh(-log_a)
        sq = 2.0 * th / (1.0 + th)
        mult = jnp.where(first, 1.0, jnp.where(sq > 0.0, sq * lax.rsqrt(sq), 0.0))
        a_val = jnp.exp(log_a)
        b_val = mult * gates[:, :HEAD_DIM] * xh
        for s in range(SEGMENTS):
            a_ref[h, s * seg_stride:s * seg_stride + seg] = a_val[s * seg:(s + 1) * seg]
            b_ref[h, s * seg_stride:s * seg_stride + seg] = b_val[s * seg:(s + 1) * seg]

    rows = [pl.ds(r, SEGMENTS, stride=seg_stride) for r in range(seg)]
    for h in range(N_HEADS):
        acc = jnp.zeros((SEGMENTS, HEAD_DIM), F32)
        prod = jnp.ones((SEGMENTS, HEAD_DIM), F32)
        for r in rows:
            ar = a_ref[h, r, :]
            acc = ar * acc + b_ref[h, r, :]
            prod = prod * ar
        segp_ref[h] = prod
        segh_ref[h] = acc
        for s in range(1, SEGMENTS):
            carry_ref[h, s:s + 1] = (segp_ref[h, s - 1:s] * carry_ref[h, s - 1:s] + segh_ref[h, s - 1:s])
        acc = carry_ref[h]
        for r in rows:
            acc = a_ref[h, r, :] * acc + b_ref[h, r, :]
            b_ref[h, r, :] = acc
        last = (SEGMENTS - 1) * seg_stride + seg - 1
        carry_ref[h, 0:1] = b_ref[h, last:last + 1]
        for s in range(SEGMENTS):
            tok_ref[s * seg:(s + 1) * seg, h * HEAD_DIM:(h + 1) * HEAD_DIM] = (
                b_ref[h, s * seg_stride:s * seg_stride + seg])
    pieces.rest()
    _Attender(pieces, proj, TOK_WIDTH, kbd_ref, vo_ref, tok_ref).rest()
    _layer_tail(x_ref, tok_ref, proj, TOK_WIDTH + XATTN_WIDTH, wout_ref, lng_ref, lnb_ref, o_ref)


def _full(arr):
    nd = arr.ndim
    return pl.BlockSpec(arr.shape, lambda b, s: (0,) * nd, pipeline_mode=pl.Buffered(1))


def _run_layer(body, x, params, kbd, vo, w_out, ln_g, ln_b, scratch, name):
    batch, seq, _ = x.shape
    tile = SEQ_TILE
    x_spec = pl.BlockSpec((1, tile, D_MODEL), lambda b, s: (b, s, 0))
    shared = [w_out.astype(BF16), ln_g.reshape(1, D_MODEL), ln_b.reshape(1, D_MODEL)]
    in_specs = ([x_spec] + [_full(p) for p in params]
                + [pl.BlockSpec((1,) + kbd.shape[1:], lambda b, s: (b, 0, 0)),
                   pl.BlockSpec((1,) + vo.shape[1:], lambda b, s: (b, 0, 0))]
                + [_full(p) for p in shared])
    common = [pltpu.VMEM((tile, params[0].shape[1]), F32),
              pltpu.VMEM((tile, D_MODEL), BF16),
              pltpu.VMEM((tile, MIX_WIDTH), F32)]
    return pl.pallas_call(
        body,
        grid=(batch, seq // tile),
        in_specs=in_specs,
        out_specs=x_spec,
        out_shape=jax.ShapeDtypeStruct(x.shape, x.dtype),
        scratch_shapes=common + scratch,
        compiler_params=pltpu.CompilerParams(
            dimension_semantics=("arbitrary", "arbitrary"), vmem_limit_bytes=VMEM_LIMIT_BYTES),
        name=name,
    )(x, *params, kbd, vo, *shared)


def kernel(x, mem, mem_kv_w, ln_g, ln_b, w_out, hgrn_lb_logits, a_w_in, a_w_s, a_b_s, b_w_in, b_norm_g,
           c_w_in, c_w_pool, c_scale, d_w_in, d_conv_w, d_conv_b, d_w_gx, d_b_gx, d_w_ga, d_b_ga,
           d_a_param):
    tile = SEQ_TILE
    assert x.shape[1] % tile == 0 and tile % CHUNK == 0
    kbd, vo = _memory_operands(mem, mem_kv_w)
    for i in range(DEPTH):
        kind, j = i % 4, i // 4
        if kind == 0:
            params = [a_w_in[j].astype(BF16), a_w_s[j], a_b_s[j].T]
            body, scratch = _gmlp_layer_kernel, []
        elif kind == 1:
            params = [b_w_in[j].astype(BF16), hgrn_lb_logits, b_norm_g[j].reshape(1, TOK_WIDTH)]
            body = functools.partial(_hgrn_layer_kernel, i)
            scratch = [pltpu.VMEM((N_HEADS, HEAD_DIM, HEAD_DIM), F32),
                       pltpu.VMEM((tile // CHUNK, N_HEADS, CHUNK, HEAD_DIM), F32),
                       pltpu.VMEM((tile // CHUNK, N_SUB, TOK_WIDTH), F32),
                       pltpu.VMEM((tile // CHUNK, len(SPLIT_HALVES) + 2, N_SUB, TOK_WIDTH), F32)]
        elif kind == 2:
            wpool = jax.scipy.linalg.block_diag(*[c_w_pool[j, g] for g in range(len(POOL_WINDOWS))])
            params = [c_w_in[j].astype(BF16), wpool.astype(BF16), c_scale[j].reshape(1, TOK_WIDTH)]
            body = _pool_layer_kernel
            scratch = [pltpu.VMEM((POOL_HALO + tile, TOK_WIDTH), F32)]
        else:
            wg = jnp.concatenate([d_w_gx[j], d_w_ga[j]], axis=-1).astype(BF16)
            bg = jnp.concatenate([d_b_gx[j], d_b_ga[j]], axis=-1)[:, None, :]
            params = [d_w_in[j].astype(BF16), d_conv_w[j], d_conv_b[j].reshape(1, TOK_WIDTH), wg, bg,
                      d_a_param[j].reshape(1, TOK_WIDTH)]
            body = _lru_layer_kernel
            scratch = [pltpu.VMEM((CONV_HALO + tile, TOK_WIDTH), F32),
                       pltpu.VMEM((N_HEADS, tile + SEGMENTS * SEGMENT_PAD, HEAD_DIM), F32),
                       pltpu.VMEM((N_HEADS, tile + SEGMENTS * SEGMENT_PAD, HEAD_DIM), F32),
                       pltpu.VMEM((N_HEADS, SEGMENTS, HEAD_DIM), F32),
                       pltpu.VMEM((N_HEADS, SEGMENTS, HEAD_DIM), F32),
                       pltpu.VMEM((N_HEADS, SEGMENTS, HEAD_DIM), F32)]
        x = _run_layer(body, x, params, kbd, vo, w_out[i], ln_g[i], ln_b[i], scratch,
                       name=f"layer{i}")
    return x
```

```python
import functools
import math

import jax
import jax.numpy as jnp
from jax import lax
from jax.experimental import pallas as pl
from jax.experimental.pallas import tpu as pltpu

F32 = jnp.float32
BF16 = jnp.bfloat16

D_MODEL = 1024
DEPTH = 4
N_MEM = 256
MIX_WIDTH = D_MODEL
XATTN_HEADS = 4
XATTN_HEAD_DIM = 64
XATTN_WIDTH = XATTN_HEADS * XATTN_HEAD_DIM
XATTN_SCALE = XATTN_HEAD_DIM ** -0.5
assert math.frexp(XATTN_SCALE)[0] == 0.5
TOK_WIDTH = MIX_WIDTH - XATTN_WIDTH
HEAD_DIM = 128
N_HEADS = TOK_WIDTH // HEAD_DIM
CHUNK = 128
SUB = 16
N_SUB = CHUNK // SUB
SPLIT_HALVES = (CHUNK // 2, CHUNK // 4, CHUNK // 8)
PAIR = 2 * HEAD_DIM
SEGMENTS = 8
SEGMENT_PAD = 4
POOL_WINDOWS = (2, 4, 8, 16)
POOL_GROUP = TOK_WIDTH // len(POOL_WINDOWS)
POOL_HALO = 16
CONV_WIDTH = 4
CONV_HALO = 8
LRU_C = 8.0
ALPHA = (2 * DEPTH) ** 0.25
LN_EPS = 1e-5
RMS_EPS = 1e-6

SEQ_TILE = 512
PROJ_PIECE = 256
TAIL_ROWS = 256
VMEM_LIMIT_BYTES = 56 * 1024 * 1024


def _dot(a, b):
    return jnp.dot(a.astype(BF16), b.astype(BF16), preferred_element_type=F32)


def _dot_nt(a, b):
    return lax.dot_general(a.astype(BF16), b.astype(BF16), (((1,), (1,)), ((), ())),
                           preferred_element_type=F32)


def _dot_tn(a, b):
    return lax.dot_general(a.astype(BF16), b.astype(BF16), (((0,), (0,)), ((), ())),
                           preferred_element_type=F32)


def _silu(t):
    return t * jax.nn.sigmoid(t)


def _kv_kernel(mem_ref, w_ref, kbd_ref, vo_ref):
    kv = jnp.dot(mem_ref[0].astype(BF16), w_ref[...], preferred_element_type=F32)
    kt = kv[:, :XATTN_WIDTH].T * XATTN_SCALE
    v = kv[:, XATTN_WIDTH:]
    row_head = lax.broadcasted_iota(jnp.int32, (XATTN_WIDTH, N_MEM), 0) // XATTN_HEAD_DIM
    lane_head = lax.broadcasted_iota(jnp.int32, (N_MEM, XATTN_WIDTH), 1) // XATTN_HEAD_DIM
    for h in range(XATTN_HEADS):
        kbd_ref[0, :, h * N_MEM:(h + 1) * N_MEM] = jnp.where(row_head == h, kt, 0.0).astype(BF16)
        vo_ref[0, h * N_MEM:(h + 1) * N_MEM, :XATTN_WIDTH] = (
            jnp.where(lane_head == h, v, 0.0).astype(BF16))
        vo_ref[0, h * N_MEM:(h + 1) * N_MEM, XATTN_WIDTH:] = (
            jnp.where(lane_head == h, 1.0, 0.0).astype(BF16))


def _memory_operands(mem, mem_kv_w):
    batch = mem.shape[0]
    return pl.pallas_call(
        _kv_kernel,
        grid=(batch,),
        in_specs=[pl.BlockSpec((1, N_MEM, D_MODEL), lambda b: (b, 0, 0)),
                  pl.BlockSpec((D_MODEL, 2 * XATTN_WIDTH), lambda b: (0, 0))],
        out_specs=[pl.BlockSpec((1, XATTN_WIDTH, XATTN_HEADS * N_MEM), lambda b: (b, 0, 0)),
                   pl.BlockSpec((1, XATTN_HEADS * N_MEM, 2 * XATTN_WIDTH), lambda b: (b, 0, 0))],
        out_shape=[jax.ShapeDtypeStruct((batch, XATTN_WIDTH, XATTN_HEADS * N_MEM), BF16),
                   jax.ShapeDtypeStruct((batch, XATTN_HEADS * N_MEM, 2 * XATTN_WIDTH), BF16)],
        name="memory_kv",
    )(mem, mem_kv_w.astype(BF16))


def _cross_attention(qx, kbd_ref, vo_ref):
    s = jnp.dot(qx.astype(BF16), kbd_ref[0], preferred_element_type=F32)
    probs = []
    for h in range(XATTN_HEADS):
        sh = s[:, h * N_MEM:(h + 1) * N_MEM]
        probs.append(jnp.exp(sh - jnp.max(sh, axis=-1, keepdims=True)).astype(BF16))
    nd = jnp.dot(jnp.concatenate(probs, axis=-1), vo_ref[0], preferred_element_type=F32)
    return nd[:, :XATTN_WIDTH] / nd[:, XATTN_WIDTH:]


def _layer_tail(x_ref, mix_ref, proj_ref, gate_start, wout_ref, lng_ref, lnb_ref, o_ref):
    for r0 in range(0, x_ref.shape[1], TAIL_ROWS):
        r = slice(r0, r0 + TAIL_ROWS)
        mixed = mix_ref[r, :] * _silu(proj_ref[r, gate_start:gate_start + MIX_WIDTH])
        y = jnp.dot(mixed.astype(BF16), wout_ref[...], preferred_element_type=F32)
        z = ALPHA * x_ref[0, r, :] + y
        zc = z - jnp.mean(z, axis=-1, keepdims=True)
        var = jnp.mean(zc * zc, axis=-1, keepdims=True)
        o_ref[0, r, :] = zc * lax.rsqrt(var + LN_EPS) * lng_ref[...] + lnb_ref[...]


class _Projector:
    def __init__(self, x_ref, win_ref, proj_ref, xb_ref):
        xb_ref[...] = x_ref[0].astype(BF16)
        self._refs = (win_ref, proj_ref, xb_ref)
        self._starts = list(range(0, proj_ref.shape[1], PROJ_PIECE))

    def _issue(self, c):
        win_ref, proj_ref, xb_ref = self._refs
        self._starts.remove(c)
        w = min(PROJ_PIECE, proj_ref.shape[1] - c)
        proj_ref[:, c:c + w] = jnp.dot(xb_ref[...], win_ref[:, c:c + w], preferred_element_type=F32)

    def step(self, count=1):
        for c in self._starts[:count]:
            self._issue(c)

    def need(self, start, width):
        for c in [c for c in self._starts if c < start + width and c + PROJ_PIECE > start]:
            self._issue(c)

    def through(self, column):
        self.need(0, column)

    def rest(self):
        self.step(len(self._starts))


class _Attender:
    def __init__(self, pieces, proj_ref, qx_start, kbd_ref, vo_ref, mix_ref):
        self._args = (pieces, proj_ref, qx_start, kbd_ref, vo_ref, mix_ref)
        self._blocks = list(range(0, mix_ref.shape[0], TAIL_ROWS))

    def step(self, count=1):
        pieces, proj_ref, qx_start, kbd_ref, vo_ref, mix_ref = self._args
        pieces.need(qx_start, XATTN_WIDTH)
        for r0 in self._blocks[:count]:
            self._blocks.remove(r0)
            r = slice(r0, r0 + TAIL_ROWS)
            mix_ref[r, TOK_WIDTH:] = _cross_attention(proj_ref[r, qx_start:qx_start + XATTN_WIDTH], kbd_ref, vo_ref)

    def rest(self):
        self.step(len(self._blocks))


def _columns(proj, start, width):
    return proj[:, start:start + width]


def _gmlp_layer_kernel(x_ref, win_ref, ws_ref, bs_ref, kbd_ref, vo_ref, wout_ref, lng_ref, lnb_ref, o_ref,
                       proj, xb_ref, tok_ref):
    pieces = _Projector(x_ref, win_ref, proj, xb_ref)
    pieces.through(TOK_WIDTH)
    u = jax.nn.gelu(_columns(proj, 0, TOK_WIDTH))
    pieces.through(2 * TOK_WIDTH)
    v = jax.nn.gelu(_columns(proj, TOK_WIDTH, TOK_WIDTH))
    rows = lax.broadcasted_iota(jnp.int32, (CHUNK, CHUNK), 0)
    cols = lax.broadcasted_iota(jnp.int32, (CHUNK, CHUNK), 1)
    causal = cols <= rows
    for g in range(N_HEADS):
        lanes = slice(g * HEAD_DIM, (g + 1) * HEAD_DIM)
        vg = v[:, lanes]
        vc = vg - jnp.mean(vg, axis=-1, keepdims=True)
        vn = (vc * lax.rsqrt(jnp.mean(vc * vc, axis=-1, keepdims=True) + LN_EPS)).astype(BF16)
        w = jnp.where(causal, ws_ref[g], 0.0).astype(BF16)
        bias = bs_ref[:, g:g + 1]
        pieces.step()
        for c in range(x_ref.shape[1] // CHUNK):
            r = slice(c * CHUNK, (c + 1) * CHUNK)
            spatial = jnp.dot(w, vn[r], preferred_element_type=F32) + bias
            tok_ref[r, lanes] = u[r, lanes] * spatial
    _Attender(pieces, proj, 2 * TOK_WIDTH, kbd_ref, vo_ref, tok_ref).rest()
    pieces.rest()
    _layer_tail(x_ref, tok_ref, proj, 2 * TOK_WIDTH + XATTN_WIDTH, wout_ref, lng_ref, lnb_ref, o_ref)


def _block_diag2(t):
    zero = jnp.zeros_like(t[:, :HEAD_DIM])
    return jnp.concatenate([jnp.concatenate([t[:, :HEAD_DIM], zero], axis=1),
                            jnp.concatenate([zero, t[:, HEAD_DIM:]], axis=1)], axis=0)


def _scale_blocks(pick, e_ref, idx, lanes):
    return jnp.concatenate(
        [pick(j)[j * SUB:(j + 1) * SUB, lanes] * e_ref[idx, j:j + 1, lanes] for j in range(N_SUB)],
        axis=0).astype(BF16)


def _hgrn_layer_kernel(layer, x_ref, win_ref, lbl_ref, ng_ref, kbd_ref, vo_ref, wout_ref, lng_ref, lnb_ref,
                       o_ref, proj, xb_ref, tok_ref, state_ref, gl_ref, s_ref, e_ref):
    @pl.when(pl.program_id(1) == 0)
    def _():
        state_ref[...] = jnp.zeros_like(state_ref)

    pieces = _Projector(x_ref, win_ref, proj, xb_ref)
    pieces.through(TOK_WIDTH)
    q = _silu(_columns(proj, 0, TOK_WIDTH))
    pieces.through(2 * TOK_WIDTH)
    f_logit = _columns(proj, TOK_WIDTH, TOK_WIDTH)

    lbl = lbl_ref[...]
    lbe = jnp.exp(lbl - jnp.max(lbl, axis=0, keepdims=True))
    lbp = lbe / jnp.sum(lbe, axis=0, keepdims=True)
    lb = jnp.sum(lbp[:layer + 1], axis=0, keepdims=True) - lbp[0:1]
    f = lb + (1.0 - lb) * jax.nn.sigmoid(f_logit)
    pieces.through(3 * TOK_WIDTH)
    log_f = jnp.log(f)
    k = 1.0 - f
    val = _columns(proj, 2 * TOK_WIDTH, TOK_WIDTH)

    tt = lax.broadcasted_iota(jnp.int32, (CHUNK, PAIR), 0)
    ss = lax.broadcasted_iota(jnp.int32, (CHUNK, PAIR), 1) % CHUNK
    level_masks = [((tt // (2 * h)) == (ss // (2 * h))) & ((tt % (2 * h)) >= h) & ((ss % (2 * h)) < h)
                   for h in SPLIT_HALVES]
    diag_mask = ((tt // SUB) == (ss // SUB)) & (ss <= tt)
    br = lax.broadcasted_iota(jnp.int32, (CHUNK, CHUNK), 0)
    bc = lax.broadcasted_iota(jnp.int32, (CHUNK, CHUNK), 1)
    block_tril = jnp.where(((br // SUB) == (bc // SUB)) & (bc <= br), 1.0, 0.0).astype(BF16)
    sub_row = lax.broadcasted_iota(jnp.int32, (N_SUB, 1), 0)
    n_lvl = len(SPLIT_HALVES)

    n_chunks = x_ref.shape[1] // CHUNK
    ql_all, kl_all, decay_all = [], [], []
    for c in range(n_chunks):
        r = slice(c * CHUNK, (c + 1) * CHUNK)
        lfc = log_f[r]
        hi = lfc.astype(BF16)
        lo = (lfc - hi.astype(F32)).astype(BF16)
        gl = (jnp.dot(block_tril, hi, preferred_element_type=F32)
              + jnp.dot(block_tril, lo, preferred_element_type=F32))
        ql_all.append(q[r] * jnp.exp(gl))
        kl_all.append(k[r] * jnp.exp(-gl))

        for h in range(N_HEADS):
            gl_ref[c, h] = gl[:, h * HEAD_DIM:(h + 1) * HEAD_DIM]
        bsum = jnp.concatenate(
            [gl_ref[c, h, pl.ds(SUB - 1, N_SUB, stride=SUB), :] for h in range(N_HEADS)], axis=1)
        incl = bsum
        for d in (1, 2, 4):
            incl = incl + jnp.where(sub_row >= d, pltpu.roll(incl, d, 0), 0.0)
        s = incl - bsum
        s_ref[c] = s
        total = s_ref[c, N_SUB - 1:N_SUB, :] + gl[CHUNK - 1:CHUNK, :]
        for li, half in enumerate(SPLIT_HALVES):
            nb = half // SUB
            boundary = s_ref[c, nb:nb + 1, :]
            for g0 in range(2 * nb, N_SUB, 2 * nb):
                boundary = jnp.where(sub_row >= g0, s_ref[c, g0 + nb:g0 + nb + 1, :], boundary)
            e_ref[c, li] = jnp.exp(-jnp.abs(s - boundary))
        e_ref[c, n_lvl] = jnp.exp(s)
        e_ref[c, n_lvl + 1] = jnp.exp(total - s)
        decay_all.append(jnp.exp(total))
        pieces.step()

    pair_lanes = [slice(p * PAIR, (p + 1) * PAIR) for p in range(N_HEADS // 2)]
    zero = jnp.zeros((HEAD_DIM, HEAD_DIM), BF16)
    attend = _Attender(pieces, proj, 3 * TOK_WIDTH, kbd_ref, vo_ref, tok_ref)
    for c in range(n_chunks):
        r = slice(c * CHUNK, (c + 1) * CHUNK)
        ql, kl, e_c = ql_all[c], kl_all[c], e_ref.at[c]
        scores = [jnp.where(diag_mask, _dot_nt(ql[:, lanes], _block_diag2(kl[:, lanes].astype(BF16))), 0.0)
                  for lanes in pair_lanes]
        for li, half in enumerate(SPLIT_HALVES):
            nb = half // SUB
            for p, lanes in enumerate(pair_lanes):
                xs = _scale_blocks(lambda j: ql if (j // nb) % 2 else kl, e_c, li, lanes)
                scores[p] = jnp.where(level_masks[li], _dot_nt(xs, _block_diag2(xs)), scores[p])
        outs, k_ends, vcs = [], [], []
        for p, lanes in enumerate(pair_lanes):
            q_dec = _scale_blocks(lambda j: ql, e_c, n_lvl, lanes)
            k_ends.append(_scale_blocks(lambda j: kl, e_c, n_lvl + 1, lanes))
            vcs.append(val[r, lanes].astype(BF16))
            state_bd = jnp.concatenate(
                [jnp.concatenate([state_ref[2 * p].astype(BF16), zero], axis=1),
                 jnp.concatenate([zero, state_ref[2 * p + 1].astype(BF16)], axis=1)], axis=0)
            outs.append(_dot(scores[p], _block_diag2(vcs[p])) + _dot_nt(q_dec, state_bd))
        for p in range(N_HEADS // 2):
            for i in range(2):
                h = 2 * p + i
                hl = slice(i * HEAD_DIM, (i + 1) * HEAD_DIM)
                head_lanes = slice(h * HEAD_DIM, (h + 1) * HEAD_DIM)
                state_ref[h] = (state_ref[h] * decay_all[c][:, head_lanes]
                                + _dot_tn(vcs[p][:, hl], k_ends[p][:, hl]))
                oh = outs[p][:, hl]
                oh = oh * lax.rsqrt(jnp.mean(oh * oh, axis=-1, keepdims=True) + RMS_EPS)
                tok_ref[r, head_lanes] = oh * ng_ref[:, head_lanes]
        pieces.step()
        if c % 2 == 0:
            attend.step()
    attend.rest()
    pieces.rest()
    _layer_tail(x_ref, tok_ref, proj, 3 * TOK_WIDTH + XATTN_WIDTH, wout_ref, lng_ref, lnb_ref, o_ref)


def _pool_layer_kernel(x_ref, win_ref, wpool_ref, scale_ref, kbd_ref, vo_ref, wout_ref, lng_ref, lnb_ref, o_ref,
                       proj, xb_ref, tok_ref, pbuf_ref):
    si = pl.program_id(1)
    tile = x_ref.shape[1]

    @pl.when(si == 0)
    def _():
        pbuf_ref[:POOL_HALO] = jnp.zeros((POOL_HALO, TOK_WIDTH), F32)

    pieces = _Projector(x_ref, win_ref, proj, xb_ref)
    pieces.through(TOK_WIDTH)
    p = _columns(proj, 0, TOK_WIDTH)

    pbuf_ref[POOL_HALO:] = p
    ext = pbuf_ref[...]
    pieces.step(2)
    sums = []
    acc = ext
    for w in POOL_WINDOWS:
        acc = acc + pltpu.roll(acc, w // 2, 0)
        sums.append(acc[POOL_HALO:])
    pbuf_ref[:POOL_HALO] = p[tile - POOL_HALO:]

    pos = si * tile + lax.broadcasted_iota(jnp.int32, (tile, 1), 0)
    chan = lax.broadcasted_iota(jnp.int32, (1, TOK_WIDTH), 1)
    pooled = None
    for gi in reversed(range(len(POOL_WINDOWS))):
        w = POOL_WINDOWS[gi]
        mean = sums[gi] * (1.0 / jnp.minimum(pos + 1, w).astype(F32))
        pooled = mean if pooled is None else jnp.where(chan < (gi + 1) * POOL_GROUP, mean, pooled)
    pieces.rest()
    tok_ref[:, :TOK_WIDTH] = _dot(pooled - p, wpool_ref[...]) * scale_ref[...]
    _Attender(pieces, proj, TOK_WIDTH, kbd_ref, vo_ref, tok_ref).rest()
    _layer_tail(x_ref, tok_ref, proj, TOK_WIDTH + XATTN_WIDTH, wout_ref, lng_ref, lnb_ref, o_ref)


def _lru_layer_kernel(x_ref, win_ref, convw_ref, convb_ref, wg_ref, bg_ref, ap_ref, kbd_ref, vo_ref, wout_ref,
                      lng_ref, lnb_ref, o_ref, proj, xb_ref, tok_ref, cbuf_ref, a_ref, b_ref, carry_ref,
                      segp_ref, segh_ref):
    si = pl.program_id(1)
    tile = x_ref.shape[1]

    @pl.when(si == 0)
    def _():
        cbuf_ref[:CONV_HALO] = jnp.zeros((CONV_HALO, TOK_WIDTH), F32)
        carry_ref[...] = jnp.zeros_like(carry_ref)

    pieces = _Projector(x_ref, win_ref, proj, xb_ref)
    seg = tile // SEGMENTS
    seg_stride = seg + SEGMENT_PAD
    ap = ap_ref[...]
    neg_softplus = -(jnp.maximum(-ap, 0.0) + jnp.log1p(jnp.exp(-jnp.abs(ap))))
    first = (si * tile + lax.broadcasted_iota(jnp.int32, (tile, 1), 0)) == 0
    for h in range(N_HEADS):
        lanes = slice(h * HEAD_DIM, (h + 1) * HEAD_DIM)
        pieces.need(h * HEAD_DIM, HEAD_DIM)
        xin = _columns(proj, h * HEAD_DIM, HEAD_DIM)
        cbuf_ref[CONV_HALO:, lanes] = xin
        xh = convw_ref[CONV_WIDTH - 1:CONV_WIDTH, lanes] * xin + convb_ref[:, lanes]
        for j in range(1, CONV_WIDTH):
            xh = xh + (convw_ref[CONV_WIDTH - 1 - j:CONV_WIDTH - j, lanes]
                       * cbuf_ref[CONV_HALO - j:CONV_HALO - j + tile, lanes])
        cbuf_ref[:CONV_HALO, lanes] = xin[tile - CONV_HALO:]
        pieces.step()
        gates = jax.nn.sigmoid(_dot(xh, wg_ref[h]) + bg_ref[h])
        log_a = LRU_C * gates[:, HEAD_DIM:] * neg_softplus[:, lanes]
        th = jnp.tanh(-log_a)
        sq = 2.0 * th / (1.0 + th)
        mult = jnp.where(first, 1.0, jnp.where(sq > 0.0, sq * lax.rsqrt(sq), 0.0))
        a_val = jnp.exp(log_a)
        b_val = mult * gates[:, :HEAD_DIM] * xh
        for s in range(SEGMENTS):
            a_ref[h, s * seg_stride:s * seg_stride + seg] = a_val[s * seg:(s + 1) * seg]
            b_ref[h, s * seg_stride:s * seg_stride + seg] = b_val[s * seg:(s + 1) * seg]

    rows = [pl.ds(r, SEGMENTS, stride=seg_stride) for r in range(seg)]
    for h in range(N_HEADS):
        acc = jnp.zeros((SEGMENTS, HEAD_DIM), F32)
        prod = jnp.ones((SEGMENTS, HEAD_DIM), F32)
        for r in rows:
            ar = a_ref[h, r, :]
            acc = ar * acc + b_ref[h, r, :]
            prod = prod * ar
        segp_ref[h] = prod
        segh_ref[h] = acc
        for s in range(1, SEGMENTS):
            carry_ref[h, s:s + 1] = (segp_ref[h, s - 1:s] * carry_ref[h, s - 1:s] + segh_ref[h, s - 1:s])
        acc = carry_ref[h]
        for r in rows:
            acc = a_ref[h, r, :] * acc + b_ref[h, r, :]
            b_ref[h, r, :] = acc
        last = (SEGMENTS - 1) * seg_stride + seg - 1
        carry_ref[h, 0:1] = b_ref[h, last:last + 1]
        for s in range(SEGMENTS):
            tok_ref[s * seg:(s + 1) * seg, h * HEAD_DIM:(h + 1) * HEAD_DIM] = (
                b_ref[h, s * seg_stride:s * seg_stride + seg])
    pieces.rest()
    _Attender(pieces, proj, TOK_WIDTH, kbd_ref, vo_ref, tok_ref).rest()
    _layer_tail(x_ref, tok_ref, proj, TOK_WIDTH + XATTN_WIDTH, wout_ref, lng_ref, lnb_ref, o_ref)


def _full(arr):
    nd = arr.ndim
    return pl.BlockSpec(arr.shape, lambda b, s: (0,) * nd, pipeline_mode=pl.Buffered(1))


def _run_layer(body, x, params, kbd, vo, w_out, ln_g, ln_b, scratch, name):
    batch, seq, _ = x.shape
    tile = SEQ_TILE
    x_spec = pl.BlockSpec((1, tile, D_MODEL), lambda b, s: (b, s, 0))
    shared = [w_out.astype(BF16), ln_g.reshape(1, D_MODEL), ln_b.reshape(1, D_MODEL)]
    in_specs = ([x_spec] + [_full(p) for p in params]
                + [pl.BlockSpec((1,) + kbd.shape[1:], lambda b, s: (b, 0, 0)),
                   pl.BlockSpec((1,) + vo.shape[1:], lambda b, s: (b, 0, 0))]
                + [_full(p) for p in shared])
    common = [pltpu.VMEM((tile, params[0].shape[1]), F32),
              pltpu.VMEM((tile, D_MODEL), BF16),
              pltpu.VMEM((tile, MIX_WIDTH), F32)]
    return pl.pallas_call(
        body,
        grid=(batch, seq // tile),
        in_specs=in_specs,
        out_specs=x_spec,
        out_shape=jax.ShapeDtypeStruct(x.shape, x.dtype),
        scratch_shapes=common + scratch,
        compiler_params=pltpu.CompilerParams(
            dimension_semantics=("arbitrary", "arbitrary"), vmem_limit_bytes=VMEM_LIMIT_BYTES),
        name=name,
    )(x, *params, kbd, vo, *shared)


def kernel(x, mem, mem_kv_w, ln_g, ln_b, w_out, hgrn_lb_logits, a_w_in, a_w_s, a_b_s, b_w_in, b_norm_g,
           c_w_in, c_w_pool, c_scale, d_w_in, d_conv_w, d_conv_b, d_w_gx, d_b_gx, d_w_ga, d_b_ga,
           d_a_param):
    tile = SEQ_TILE
    assert x.shape[1] % tile == 0 and tile % CHUNK == 0
    kbd, vo = _memory_operands(mem, mem_kv_w)
    for i in range(DEPTH):
        kind, j = i % 4, i // 4
        if kind == 0:
            params = [a_w_in[j].astype(BF16), a_w_s[j], a_b_s[j].T]
            body, scratch = _gmlp_layer_kernel, []
        elif kind == 1:
            params = [b_w_in[j].astype(BF16), hgrn_lb_logits, b_norm_g[j].reshape(1, TOK_WIDTH)]
            body = functools.partial(_hgrn_layer_kernel, i)
            scratch = [pltpu.VMEM((N_HEADS, HEAD_DIM, HEAD_DIM), F32),
                       pltpu.VMEM((tile // CHUNK, N_HEADS, CHUNK, HEAD_DIM), F32),
                       pltpu.VMEM((tile // CHUNK, N_SUB, TOK_WIDTH), F32),
                       pltpu.VMEM((tile // CHUNK, len(SPLIT_HALVES) + 2, N_SUB, TOK_WIDTH), F32)]
        elif kind == 2:
            wpool = jax.scipy.linalg.block_diag(*[c_w_pool[j, g] for g in range(len(POOL_WINDOWS))])
            params = [c_w_in[j].astype(BF16), wpool.astype(BF16), c_scale[j].reshape(1, TOK_WIDTH)]
            body = _pool_layer_kernel
            scratch = [pltpu.VMEM((POOL_HALO + tile, TOK_WIDTH), F32)]
        else:
            wg = jnp.concatenate([d_w_gx[j], d_w_ga[j]], axis=-1).astype(BF16)
            bg = jnp.concatenate([d_b_gx[j], d_b_ga[j]], axis=-1)[:, None, :]
            params = [d_w_in[j].astype(BF16), d_conv_w[j], d_conv_b[j].reshape(1, TOK_WIDTH), wg, bg,
                      d_a_param[j].reshape(1, TOK_WIDTH)]
            body = _lru_layer_kernel
            scratch = [pltpu.VMEM((CONV_HALO + tile, TOK_WIDTH), F32),
                       pltpu.VMEM((N_HEADS, tile + SEGMENTS * SEGMENT_PAD, HEAD_DIM), F32),
                       pltpu.VMEM((N_HEADS, tile + SEGMENTS * SEGMENT_PAD, HEAD_DIM), F32),
                       pltpu.VMEM((N_HEADS, SEGMENTS, HEAD_DIM), F32),
                       pltpu.VMEM((N_HEADS, SEGMENTS, HEAD_DIM), F32),
                       pltpu.VMEM((N_HEADS, SEGMENTS, HEAD_DIM), F32)]
        x = _run_layer(body, x, params, kbd, vo, w_out[i], ln_g[i], ln_b[i], scratch,
                       name=f"layer{i}")
    return x
```

```python
import functools
import math

import jax
import jax.numpy as jnp
from jax import lax
from jax.experimental import pallas as pl
from jax.experimental.pallas import tpu as pltpu

F32 = jnp.float32
BF16 = jnp.bfloat16

D_MODEL = 1024
DEPTH = 4
N_MEM = 256
MIX_WIDTH = D_MODEL
XATTN_HEADS = 4
XATTN_HEAD_DIM = 64
XATTN_WIDTH = XATTN_HEADS * XATTN_HEAD_DIM
XATTN_SCALE = XATTN_HEAD_DIM ** -0.5
assert math.frexp(XATTN_SCALE)[0] == 0.5
TOK_WIDTH = MIX_WIDTH - XATTN_WIDTH
HEAD_DIM = 128
N_HEADS = TOK_WIDTH // HEAD_DIM
CHUNK = 128
SUB = 16
N_SUB = CHUNK // SUB
SPLIT_HALVES = (CHUNK // 2, CHUNK // 4, CHUNK // 8)
PAIR = 2 * HEAD_DIM
SEGMENTS = 8
SEGMENT_PAD = 4
POOL_WINDOWS = (2, 4, 8, 16)
POOL_GROUP = TOK_WIDTH // len(POOL_WINDOWS)
POOL_HALO = 16
CONV_WIDTH = 4
CONV_HALO = 8
LRU_C = 8.0
ALPHA = (2 * DEPTH) ** 0.25
LN_EPS = 1e-5
RMS_EPS = 1e-6

SEQ_TILE = 512
PROJ_PIECE = 256
TAIL_ROWS = 256
VMEM_LIMIT_BYTES = 56 * 1024 * 1024


def _dot(a, b):
    return jnp.dot(a.astype(BF16), b.astype(BF16), preferred_element_type=F32)


def _dot_nt(a, b):
    return lax.dot_general(a.astype(BF16), b.astype(BF16), (((1,), (1,)), ((), ())),
                           preferred_element_type=F32)


def _dot_tn(a, b):
    return lax.dot_general(a.astype(BF16), b.astype(BF16), (((0,), (0,)), ((), ())),
                           preferred_element_type=F32)


def _silu(t):
    return t * jax.nn.sigmoid(t)


def _kv_kernel(mem_ref, w_ref, kbd_ref, vo_ref):
    kv = jnp.dot(mem_ref[0].astype(BF16), w_ref[...], preferred_element_type=F32)
    kt = kv[:, :XATTN_WIDTH].T * XATTN_SCALE
    v = kv[:, XATTN_WIDTH:]
    row_head = lax.broadcasted_iota(jnp.int32, (XATTN_WIDTH, N_MEM), 0) // XATTN_HEAD_DIM
    lane_head = lax.broadcasted_iota(jnp.int32, (N_MEM, XATTN_WIDTH), 1) // XATTN_HEAD_DIM
    for h in range(XATTN_HEADS):
        kbd_ref[0, :, h * N_MEM:(h + 1) * N_MEM] = jnp.where(row_head == h, kt, 0.0).astype(BF16)
        vo_ref[0, h * N_MEM:(h + 1) * N_MEM, :XATTN_WIDTH] = (
            jnp.where(lane_head == h, v, 0.0).astype(BF16))
        vo_ref[0, h * N_MEM:(h + 1) * N_MEM, XATTN_WIDTH:] = (
            jnp.where(lane_head == h, 1.0, 0.0).astype(BF16))


def _memory_operands(mem, mem_kv_w):
    batch = mem.shape[0]
    return pl.pallas_call(
        _kv_kernel,
        grid=(batch,),
        in_specs=[pl.BlockSpec((1, N_MEM, D_MODEL), lambda b: (b, 0, 0)),
                  pl.BlockSpec((D_MODEL, 2 * XATTN_WIDTH), lambda b: (0, 0))],
        out_specs=[pl.BlockSpec((1, XATTN_WIDTH, XATTN_HEADS * N_MEM), lambda b: (b, 0, 0)),
                   pl.BlockSpec((1, XATTN_HEADS * N_MEM, 2 * XATTN_WIDTH), lambda b: (b, 0, 0))],
        out_shape=[jax.ShapeDtypeStruct((batch, XATTN_WIDTH, XATTN_HEADS * N_MEM), BF16),
                   jax.ShapeDtypeStruct((batch, XATTN_HEADS * N_MEM, 2 * XATTN_WIDTH), BF16)],
        name="memory_kv",
    )(mem, mem_kv_w.astype(BF16))


def _cross_attention(qx, kbd_ref, vo_ref):
    s = jnp.dot(qx.astype(BF16), kbd_ref[0], preferred_element_type=F32)
    probs = []
    for h in range(XATTN_HEADS):
        sh = s[:, h * N_MEM:(h + 1) * N_MEM]
        probs.append(jnp.exp(sh - jnp.max(sh, axis=-1, keepdims=True)).astype(BF16))
    nd = jnp.dot(jnp.concatenate(probs, axis=-1), vo_ref[0], preferred_element_type=F32)
    return nd[:, :XATTN_WIDTH] / nd[:, XATTN_WIDTH:]


def _layer_tail(x_ref, mix_ref, proj_ref, gate_start, wout_ref, lng_ref, lnb_ref, o_ref):
    for r0 in range(0, x_ref.shape[1], TAIL_ROWS):
        r = slice(r0, r0 + TAIL_ROWS)
        mixed = mix_ref[r, :] * _silu(proj_ref[r, gate_start:gate_start + MIX_WIDTH])
        y = jnp.dot(mixed.astype(BF16), wout_ref[...], preferred_element_type=F32)
        z = ALPHA * x_ref[0, r, :] + y
        zc = z - jnp.mean(z, axis=-1, keepdims=True)
        var = jnp.mean(zc * zc, axis=-1, keepdims=True)
        o_ref[0, r, :] = zc * lax.rsqrt(var + LN_EPS) * lng_ref[...] + lnb_ref[...]


class _Projector:
    def __init__(self, x_ref, win_ref, proj_ref, xb_ref):
        xb_ref[...] = x_ref[0].astype(BF16)
        self._refs = (win_ref, proj_ref, xb_ref)
        self._starts = list(range(0, proj_ref.shape[1], PROJ_PIECE))

    def _issue(self, c):
        win_ref, proj_ref, xb_ref = self._refs
        self._starts.remove(c)
        w = min(PROJ_PIECE, proj_ref.shape[1] - c)
        proj_ref[:, c:c + w] = jnp.dot(xb_ref[...], win_ref[:, c:c + w], preferred_element_type=F32)

    def step(self, count=1):
        for c in self._starts[:count]:
            self._issue(c)

    def need(self, start, width):
        for c in [c for c in self._starts if c < start + width and c + PROJ_PIECE > start]:
            self._issue(c)

    def through(self, column):
        self.need(0, column)

    def rest(self):
        self.step(len(self._starts))


class _Attender:
    def __init__(self, pieces, proj_ref, qx_start, kbd_ref, vo_ref, mix_ref):
        self._args = (pieces, proj_ref, qx_start, kbd_ref, vo_ref, mix_ref)
        self._blocks = list(range(0, mix_ref.shape[0], TAIL_ROWS))

    def step(self, count=1):
        pieces, proj_ref, qx_start, kbd_ref, vo_ref, mix_ref = self._args
        pieces.need(qx_start, XATTN_WIDTH)
        for r0 in self._blocks[:count]:
            self._blocks.remove(r0)
            r = slice(r0, r0 + TAIL_ROWS)
            mix_ref[r, TOK_WIDTH:] = _cross_attention(proj_ref[r, qx_start:qx_start + XATTN_WIDTH], kbd_ref, vo_ref)

    def rest(self):
        self.step(len(self._blocks))


def _columns(proj, start, width):
    return proj[:, start:start + width]


def _gmlp_layer_kernel(x_ref, win_ref, ws_ref, bs_ref, kbd_ref, vo_ref, wout_ref, lng_ref, lnb_ref, o_ref,
                       proj, xb_ref, tok_ref):
    pieces = _Projector(x_ref, win_ref, proj, xb_ref)
    pieces.through(TOK_WIDTH)
    u = jax.nn.gelu(_columns(proj, 0, TOK_WIDTH))
    pieces.through(2 * TOK_WIDTH)
    v = jax.nn.gelu(_columns(proj, TOK_WIDTH, TOK_WIDTH))
    rows = lax.broadcasted_iota(jnp.int32, (CHUNK, CHUNK), 0)
    cols = lax.broadcasted_iota(jnp.int32, (CHUNK, CHUNK), 1)
    causal = cols <= rows
    for g in range(N_HEADS):
        lanes = slice(g * HEAD_DIM, (g + 1) * HEAD_DIM)
        vg = v[:, lanes]
        vc = vg - jnp.mean(vg, axis=-1, keepdims=True)
        vn = (vc * lax.rsqrt(jnp.mean(vc * vc, axis=-1, keepdims=True) + LN_EPS)).astype(BF16)
        w = jnp.where(causal, ws_ref[g], 0.0).astype(BF16)
        bias = bs_ref[:, g:g + 1]
        pieces.step()
        for c in range(x_ref.shape[1] // CHUNK):
            r = slice(c * CHUNK, (c + 1) * CHUNK)
            spatial = jnp.dot(w, vn[r], preferred_element_type=F32) + bias
            tok_ref[r, lanes] = u[r, lanes] * spatial
    _Attender(pieces, proj, 2 * TOK_WIDTH, kbd_ref, vo_ref, tok_ref).rest()
    pieces.rest()
    _layer_tail(x_ref, tok_ref, proj, 2 * TOK_WIDTH + XATTN_WIDTH, wout_ref, lng_ref, lnb_ref, o_ref)


def _block_diag2(t):
    zero = jnp.zeros_like(t[:, :HEAD_DIM])
    return jnp.concatenate([jnp.concatenate([t[:, :HEAD_DIM], zero], axis=1),
                            jnp.concatenate([zero, t[:, HEAD_DIM:]], axis=1)], axis=0)


def _scale_blocks(pick, e_ref, idx, lanes):
    return jnp.concatenate(
        [pick(j)[j * SUB:(j + 1) * SUB, lanes] * e_ref[idx, j:j + 1, lanes] for j in range(N_SUB)],
        axis=0).astype(BF16)


def _hgrn_layer_kernel(layer, x_ref, win_ref, lbl_ref, ng_ref, kbd_ref, vo_ref, wout_ref, lng_ref, lnb_ref,
                       o_ref, proj, xb_ref, tok_ref, state_ref, gl_ref, s_ref, e_ref):
    @pl.when(pl.program_id(1) == 0)
    def _():
        state_ref[...] = jnp.zeros_like(state_ref)

    pieces = _Projector(x_ref, win_ref, proj, xb_ref)
    pieces.through(TOK_WIDTH)
    q = _silu(_columns(proj, 0, TOK_WIDTH))
    pieces.through(2 * TOK_WIDTH)
    f_logit = _columns(proj, TOK_WIDTH, TOK_WIDTH)

    lbl = lbl_ref[...]
    lbe = jnp.exp(lbl - jnp.max(lbl, axis=0, keepdims=True))
    lbp = lbe / jnp.sum(lbe, axis=0, keepdims=True)
    lb = jnp.sum(lbp[:layer + 1], axis=0, keepdims=True) - lbp[0:1]
    f = lb + (1.0 - lb) * jax.nn.sigmoid(f_logit)
    pieces.through(3 * TOK_WIDTH)
    log_f = jnp.log(f)
    k = 1.0 - f
    val = _columns(proj, 2 * TOK_WIDTH, TOK_WIDTH)

    tt = lax.broadcasted_iota(jnp.int32, (CHUNK, PAIR), 0)
    ss = lax.broadcasted_iota(jnp.int32, (CHUNK, PAIR), 1) % CHUNK
    level_masks = [((tt // (2 * h)) == (ss // (2 * h))) & ((tt % (2 * h)) >= h) & ((ss % (2 * h)) < h)
                   for h in SPLIT_HALVES]
    diag_mask = ((tt // SUB) == (ss // SUB)) & (ss <= tt)
    br = lax.broadcasted_iota(jnp.int32, (CHUNK, CHUNK), 0)
    bc = lax.broadcasted_iota(jnp.int32, (CHUNK, CHUNK), 1)
    block_tril = jnp.where(((br // SUB) == (bc // SUB)) & (bc <= br), 1.0, 0.0).astype(BF16)
    sub_row = lax.broadcasted_iota(jnp.int32, (N_SUB, 1), 0)
    n_lvl = len(SPLIT_HALVES)

    n_chunks = x_ref.shape[1] // CHUNK
    ql_all, kl_all, decay_all = [], [], []
    for c in range(n_chunks):
        r = slice(c * CHUNK, (c + 1) * CHUNK)
        lfc = log_f[r]
        hi = lfc.astype(BF16)
        lo = (lfc - hi.astype(F32)).astype(BF16)
        gl = (jnp.dot(block_tril, hi, preferred_element_type=F32)
              + jnp.dot(block_tril, lo, preferred_element_type=F32))
        ql_all.append(q[r] * jnp.exp(gl))
        kl_all.append(k[r] * jnp.exp(-gl))

        for h in range(N_HEADS):
            gl_ref[c, h] = gl[:, h * HEAD_DIM:(h + 1) * HEAD_DIM]
        bsum = jnp.concatenate(
            [gl_ref[c, h, pl.ds(SUB - 1, N_SUB, stride=SUB), :] for h in range(N_HEADS)], axis=1)
        incl = bsum
        for d in (1, 2, 4):
            incl = incl + jnp.where(sub_row >= d, pltpu.roll(incl, d, 0), 0.0)
        s = incl - bsum
        s_ref[c] = s
        total = s_ref[c, N_SUB - 1:N_SUB, :] + gl[CHUNK - 1:CHUNK, :]
        for li, half in enumerate(SPLIT_HALVES):
            nb = half // SUB
            boundary = s_ref[c, nb:nb + 1, :]
            for g0 in range(2 * nb, N_SUB, 2 * nb):
                boundary = jnp.where(sub_row >= g0, s_ref[c, g0 + nb:g0 + nb + 1, :], boundary)
            e_ref[c, li] = jnp.exp(-jnp.abs(s - boundary))
        e_ref[c, n_lvl] = jnp.exp(s)
        e_ref[c, n_lvl + 1] = jnp.exp(total - s)
        decay_all.append(jnp.exp(total))
        pieces.step()

    pair_lanes = [slice(p * PAIR, (p + 1) * PAIR) for p in range(N_HEADS // 2)]
    zero = jnp.zeros((HEAD_DIM, HEAD_DIM), BF16)
    attend = _Attender(pieces, proj, 3 * TOK_WIDTH, kbd_ref, vo_ref, tok_ref)
    for c in range(n_chunks):
        r = slice(c * CHUNK, (c + 1) * CHUNK)
        ql, kl, e_c = ql_all[c], kl_all[c], e_ref.at[c]
        scores = [jnp.where(diag_mask, _dot_nt(ql[:, lanes], _block_diag2(kl[:, lanes].astype(BF16))), 0.0)
                  for lanes in pair_lanes]
        for li, half in enumerate(SPLIT_HALVES):
            nb = half // SUB
            for p, lanes in enumerate(pair_lanes):
                xs = _scale_blocks(lambda j: ql if (j // nb) % 2 else kl, e_c, li, lanes)
                scores[p] = jnp.where(level_masks[li], _dot_nt(xs, _block_diag2(xs)), scores[p])
        outs, k_ends, vcs = [], [], []
        for p, lanes in enumerate(pair_lanes):
            q_dec = _scale_blocks(lambda j: ql, e_c, n_lvl, lanes)
            k_ends.append(_scale_blocks(lambda j: kl, e_c, n_lvl + 1, lanes))
            vcs.append(val[r, lanes].astype(BF16))
            state_bd = jnp.concatenate(
                [jnp.concatenate([state_ref[2 * p].astype(BF16), zero], axis=1),
                 jnp.concatenate([zero, state_ref[2 * p + 1].astype(BF16)], axis=1)], axis=0)
            outs.append(_dot(scores[p], _block_diag2(vcs[p])) + _dot_nt(q_dec, state_bd))
        for p in range(N_HEADS // 2):
            for i in range(2):
                h = 2 * p + i
                hl = slice(i * HEAD_DIM, (i + 1) * HEAD_DIM)
                head_lanes = slice(h * HEAD_DIM, (h + 1) * HEAD_DIM)
                state_ref[h] = (state_ref[h] * decay_all[c][:, head_lanes]
                                + _dot_tn(vcs[p][:, hl], k_ends[p][:, hl]))
                oh = outs[p][:, hl]
                oh = oh * lax.rsqrt(jnp.mean(oh * oh, axis=-1, keepdims=True) + RMS_EPS)
                tok_ref[r, head_lanes] = oh * ng_ref[:, head_lanes]
        pieces.step()
        if c % 2 == 0:
            attend.step()
    attend.rest()
    pieces.rest()
    _layer_tail(x_ref, tok_ref, proj, 3 * TOK_WIDTH + XATTN_WIDTH, wout_ref, lng_ref, lnb_ref, o_ref)


def _pool_layer_kernel(x_ref, win_ref, wpool_ref, scale_ref, kbd_ref, vo_ref, wout_ref, lng_ref, lnb_ref, o_ref,
                       proj, xb_ref, tok_ref, pbuf_ref):
    si = pl.program_id(1)
    tile = x_ref.shape[1]

    @pl.when(si == 0)
    def _():
        pbuf_ref[:POOL_HALO] = jnp.zeros((POOL_HALO, TOK_WIDTH), F32)

    pieces = _Projector(x_ref, win_ref, proj, xb_ref)
    pieces.through(TOK_WIDTH)
    p = _columns(proj, 0, TOK_WIDTH)

    pbuf_ref[POOL_HALO:] = p
    ext = pbuf_ref[...]
    pieces.step(2)
    sums = []
    acc = ext
    for w in POOL_WINDOWS:
        acc = acc + pltpu.roll(acc, w // 2, 0)
        sums.append(acc[POOL_HALO:])
    pbuf_ref[:POOL_HALO] = p[tile - POOL_HALO:]

    pos = si * tile + lax.broadcasted_iota(jnp.int32, (tile, 1), 0)
    chan = lax.broadcasted_iota(jnp.int32, (1, TOK_WIDTH), 1)
    pooled = None
    for gi in reversed(range(len(POOL_WINDOWS))):
        w = POOL_WINDOWS[gi]
        mean = sums[gi] * (1.0 / jnp.minimum(pos + 1, w).astype(F32))
        pooled = mean if pooled is None else jnp.where(chan < (gi + 1) * POOL_GROUP, mean, pooled)
    pieces.rest()
    tok_ref[:, :TOK_WIDTH] = _dot(pooled - p, wpool_ref[...]) * scale_ref[...]
    _Attender(pieces, proj, TOK_WIDTH, kbd_ref, vo_ref, tok_ref).rest()
    _layer_tail(x_ref, tok_ref, proj, TOK_WIDTH + XATTN_WIDTH, wout_ref, lng_ref, lnb_ref, o_ref)


def _lru_layer_kernel(x_ref, win_ref, convw_ref, convb_ref, wg_ref, bg_ref, ap_ref, kbd_ref, vo_ref, wout_ref,
                      lng_ref, lnb_ref, o_ref, proj, xb_ref, tok_ref, cbuf_ref, a_ref, b_ref, carry_ref,
                      segp_ref, segh_ref):
    si = pl.program_id(1)
    tile = x_ref.shape[1]

    @pl.when(si == 0)
    def _():
        cbuf_ref[:CONV_HALO] = jnp.zeros((CONV_HALO, TOK_WIDTH), F32)
        carry_ref[...] = jnp.zeros_like(carry_ref)

    pieces = _Projector(x_ref, win_ref, proj, xb_ref)
    seg = tile // SEGMENTS
    seg_stride = seg + SEGMENT_PAD
    ap = ap_ref[...]
    neg_softplus = -(jnp.maximum(-ap, 0.0) + jnp.log1p(jnp.exp(-jnp.abs(ap))))
    first = (si * tile + lax.broadcasted_iota(jnp.int32, (tile, 1), 0)) == 0
    for h in range(N_HEADS):
        lanes = slice(h * HEAD_DIM, (h + 1) * HEAD_DIM)
        pieces.need(h * HEAD_DIM, HEAD_DIM)
        xin = _columns(proj, h * HEAD_DIM, HEAD_DIM)
        cbuf_ref[CONV_HALO:, lanes] = xin
        xh = convw_ref[CONV_WIDTH - 1:CONV_WIDTH, lanes] * xin + convb_ref[:, lanes]
        for j in range(1, CONV_WIDTH):
            xh = xh + (convw_ref[CONV_WIDTH - 1 - j:CONV_WIDTH - j, lanes]
                       * cbuf_ref[CONV_HALO - j:CONV_HALO - j + tile, lanes])
        cbuf_ref[:CONV_HALO, lanes] = xin[tile - CONV_HALO:]
        pieces.step()
        gates = jax.nn.sigmoid(_dot(xh, wg_ref[h]) + bg_ref[h])
        log_a = LRU_C * gates[:, HEAD_DIM:] * neg_softplus[:, lanes]
        th = jnp.tanh(-log_a)
        sq = 2.0 * th / (1.0 + th)
        mult = jnp.where(first, 1.0, jnp.where(sq > 0.0, sq * lax.rsqrt(sq), 0.0))
        a_val = jnp.exp(log_a)
        b_val = mult * gates[:, :HEAD_DIM] * xh
        for s in range(SEGMENTS):
            a_ref[h, s * seg_stride:s * seg_stride + seg] = a_val[s * seg:(s + 1) * seg]
            b_ref[h, s * seg_stride:s * seg_stride + seg] = b_val[s * seg:(s + 1) * seg]

    rows = [pl.ds(r, SEGMENTS, stride=seg_stride) for r in range(seg)]
    for h in range(N_HEADS):
        acc = jnp.zeros((SEGMENTS, HEAD_DIM), F32)
        prod = jnp.ones((SEGMENTS, HEAD_DIM), F32)
        for r in rows:
            ar = a_ref[h, r, :]
            acc = ar * acc + b_ref[h, r, :]
            prod = prod * ar
        segp_ref[h] = prod
        segh_ref[h] = acc
        for s in range(1, SEGMENTS):
            carry_ref[h, s:s + 1] = (segp_ref[h, s - 1:s] * carry_ref[h, s - 1:s] + segh_ref[h, s - 1:s])
        acc = carry_ref[h]
        for r in rows:
            acc = a_ref[h, r, :] * acc + b_ref[h, r, :]
            b_ref[h, r, :] = acc
        last = (SEGMENTS - 1) * seg_stride + seg - 1
        carry_ref[h, 0:1] = b_ref[h, last:last + 1]
        for s in range(SEGMENTS):
            tok_ref[s * seg:(s + 1) * seg, h * HEAD_DIM:(h + 1) * HEAD_DIM] = (
                b_ref[h, s * seg_stride:s * seg_stride + seg])
    pieces.rest()
    _Attender(pieces, proj, TOK_WIDTH, kbd_ref, vo_ref, tok_ref).rest()
    _layer_tail(x_ref, tok_ref, proj, TOK_WIDTH + XATTN_WIDTH, wout_ref, lng_ref, lnb_ref, o_ref)


def _full(arr):
    nd = arr.ndim
    return pl.BlockSpec(arr.shape, lambda b, s: (0,) * nd, pipeline_mode=pl.Buffered(1))


def _with_rounded_weights(body, n_params, x_ref, win_ref, *refs):
    others = refs[:n_params]
    kbd_ref, vo_ref, wout_ref, lng_ref, lnb_ref, o_ref, win_s, wout_s = refs[n_params:n_params + 8]
    scratch = refs[n_params + 8:]

    @pl.when((pl.program_id(0) == 0) & (pl.program_id(1) == 0))
    def _():
        for c in range(0, win_s.shape[1], PROJ_PIECE):
            win_s[:, c:c + PROJ_PIECE] = win_ref[0, :, c:c + PROJ_PIECE].astype(BF16)
        for c in range(0, wout_s.shape[1], PROJ_PIECE):
            wout_s[:, c:c + PROJ_PIECE] = wout_ref[0, :, c:c + PROJ_PIECE].astype(BF16)

    body(x_ref, win_s, *others, kbd_ref, vo_ref, wout_s, lng_ref, lnb_ref, o_ref, *scratch)


def _run_layer(body, x, w_in, j, params, kbd, vo, w_out, i, ln_g, ln_b, scratch, name):
    batch, seq, _ = x.shape
    tile = SEQ_TILE
    in_width = w_in.shape[2]
    x_spec = pl.BlockSpec((1, tile, D_MODEL), lambda b, s: (b, s, 0))
    norm = [ln_g[i].reshape(1, D_MODEL), ln_b[i].reshape(1, D_MODEL)]
    in_specs = ([x_spec,
                 pl.BlockSpec((1, D_MODEL, in_width), lambda b, s: (j, 0, 0), pipeline_mode=pl.Buffered(1))]
                + [_full(p) for p in params]
                + [pl.BlockSpec((1,) + kbd.shape[1:], lambda b, s: (b, 0, 0)),
                   pl.BlockSpec((1,) + vo.shape[1:], lambda b, s: (b, 0, 0)),
                   pl.BlockSpec((1, MIX_WIDTH, D_MODEL), lambda b, s: (i, 0, 0), pipeline_mode=pl.Buffered(1))]
                + [_full(p) for p in norm])
    common = [pltpu.VMEM((D_MODEL, in_width), BF16),
              pltpu.VMEM((MIX_WIDTH, D_MODEL), BF16),
              pltpu.VMEM((tile, in_width), F32),
              pltpu.VMEM((tile, D_MODEL), BF16),
              pltpu.VMEM((tile, MIX_WIDTH), F32)]
    return pl.pallas_call(
        functools.partial(_with_rounded_weights, body, len(params)),
        grid=(batch, seq // tile),
        in_specs=in_specs,
        out_specs=x_spec,
        out_shape=jax.ShapeDtypeStruct(x.shape, x.dtype),
        scratch_shapes=common + scratch,
        compiler_params=pltpu.CompilerParams(
            dimension_semantics=("arbitrary", "arbitrary"), vmem_limit_bytes=VMEM_LIMIT_BYTES),
        name=name,
    )(x, w_in, *params, kbd, vo, w_out, *norm)


def kernel(x, mem, mem_kv_w, ln_g, ln_b, w_out, hgrn_lb_logits, a_w_in, a_w_s, a_b_s, b_w_in, b_norm_g,
           c_w_in, c_w_pool, c_scale, d_w_in, d_conv_w, d_conv_b, d_w_gx, d_b_gx, d_w_ga, d_b_ga,
           d_a_param):
    tile = SEQ_TILE
    assert x.shape[1] % tile == 0 and tile % CHUNK == 0
    kbd, vo = _memory_operands(mem, mem_kv_w)
    for i in range(DEPTH):
        kind, j = i % 4, i // 4
        if kind == 0:
            w_in, params = a_w_in, [a_w_s[j], a_b_s[j].T]
            body, scratch = _gmlp_layer_kernel, []
        elif kind == 1:
            w_in, params = b_w_in, [hgrn_lb_logits, b_norm_g[j].reshape(1, TOK_WIDTH)]
            body = functools.partial(_hgrn_layer_kernel, i)
            scratch = [pltpu.VMEM((N_HEADS, HEAD_DIM, HEAD_DIM), F32),
                       pltpu.VMEM((tile // CHUNK, N_HEADS, CHUNK, HEAD_DIM), F32),
                       pltpu.VMEM((tile // CHUNK, N_SUB, TOK_WIDTH), F32),
                       pltpu.VMEM((tile // CHUNK, len(SPLIT_HALVES) + 2, N_SUB, TOK_WIDTH), F32)]
        elif kind == 2:
            wpool = jax.scipy.linalg.block_diag(*[c_w_pool[j, g] for g in range(len(POOL_WINDOWS))])
            w_in, params = c_w_in, [wpool.astype(BF16), c_scale[j].reshape(1, TOK_WIDTH)]
            body = _pool_layer_kernel
            scratch = [pltpu.VMEM((POOL_HALO + tile, TOK_WIDTH), F32)]
        else:
            wg = jnp.concatenate([d_w_gx[j], d_w_ga[j]], axis=-1).astype(BF16)
            bg = jnp.concatenate([d_b_gx[j], d_b_ga[j]], axis=-1)[:, None, :]
            w_in = d_w_in
            params = [d_conv_w[j], d_conv_b[j].reshape(1, TOK_WIDTH), wg, bg, d_a_param[j].reshape(1, TOK_WIDTH)]
            body = _lru_layer_kernel
            scratch = [pltpu.VMEM((CONV_HALO + tile, TOK_WIDTH), F32),
                       pltpu.VMEM((N_HEADS, tile + SEGMENTS * SEGMENT_PAD, HEAD_DIM), F32),
                       pltpu.VMEM((N_HEADS, tile + SEGMENTS * SEGMENT_PAD, HEAD_DIM), F32),
                       pltpu.VMEM((N_HEADS, SEGMENTS, HEAD_DIM), F32),
                       pltpu.VMEM((N_HEADS, SEGMENTS, HEAD_DIM), F32),
                       pltpu.VMEM((N_HEADS, SEGMENTS, HEAD_DIM), F32)]
        x = _run_layer(body, x, w_in, j, params, kbd, vo, w_out, i, ln_g, ln_b, scratch, name=f"layer{i}")
    return x
```

```python
import functools
import math

import jax
import jax.numpy as jnp
from jax import lax
from jax.experimental import pallas as pl
from jax.experimental.pallas import tpu as pltpu

F32 = jnp.float32
BF16 = jnp.bfloat16

D_MODEL = 1024
DEPTH = 4
N_MEM = 256
MIX_WIDTH = D_MODEL
XATTN_HEADS = 4
XATTN_HEAD_DIM = 64
XATTN_WIDTH = XATTN_HEADS * XATTN_HEAD_DIM
XATTN_SCALE = XATTN_HEAD_DIM ** -0.5
assert math.frexp(XATTN_SCALE)[0] == 0.5
TOK_WIDTH = MIX_WIDTH - XATTN_WIDTH
HEAD_DIM = 128
N_HEADS = TOK_WIDTH // HEAD_DIM
CHUNK = 128
SUB = 16
N_SUB = CHUNK // SUB
SPLIT_HALVES = (CHUNK // 2, CHUNK // 4, CHUNK // 8)
PAIR = 2 * HEAD_DIM
SEGMENTS = 8
SEGMENT_PAD = 4
POOL_WINDOWS = (2, 4, 8, 16)
POOL_GROUP = TOK_WIDTH // len(POOL_WINDOWS)
POOL_HALO = 16
CONV_WIDTH = 4
CONV_HALO = 8
LRU_C = 8.0
ALPHA = (2 * DEPTH) ** 0.25
LN_EPS = 1e-5
RMS_EPS = 1e-6

SEQ_TILE = 512
PROJ_PIECE = 256
TAIL_ROWS = 256
VMEM_LIMIT_BYTES = 56 * 1024 * 1024


def _dot(a, b):
    return jnp.dot(a.astype(BF16), b.astype(BF16), preferred_element_type=F32)


def _dot_nt(a, b):
    return lax.dot_general(a.astype(BF16), b.astype(BF16), (((1,), (1,)), ((), ())),
                           preferred_element_type=F32)


def _dot_tn(a, b):
    return lax.dot_general(a.astype(BF16), b.astype(BF16), (((0,), (0,)), ((), ())),
                           preferred_element_type=F32)


def _silu(t):
    return t * jax.nn.sigmoid(t)


def _kv_kernel(mem_ref, w_ref, kbd_ref, vo_ref):
    kv = jnp.dot(mem_ref[0].astype(BF16), w_ref[...], preferred_element_type=F32)
    kt = kv[:, :XATTN_WIDTH].T * XATTN_SCALE
    v = kv[:, XATTN_WIDTH:]
    row_head = lax.broadcasted_iota(jnp.int32, (XATTN_WIDTH, N_MEM), 0) // XATTN_HEAD_DIM
    lane_head = lax.broadcasted_iota(jnp.int32, (N_MEM, XATTN_WIDTH), 1) // XATTN_HEAD_DIM
    for h in range(XATTN_HEADS):
        kbd_ref[0, :, h * N_MEM:(h + 1) * N_MEM] = jnp.where(row_head == h, kt, 0.0).astype(BF16)
        vo_ref[0, h * N_MEM:(h + 1) * N_MEM, :XATTN_WIDTH] = (
            jnp.where(lane_head == h, v, 0.0).astype(BF16))
        vo_ref[0, h * N_MEM:(h + 1) * N_MEM, XATTN_WIDTH:] = (
            jnp.where(lane_head == h, 1.0, 0.0).astype(BF16))


def _memory_operands(mem, mem_kv_w):
    batch = mem.shape[0]
    return pl.pallas_call(
        _kv_kernel,
        grid=(batch,),
        in_specs=[pl.BlockSpec((1, N_MEM, D_MODEL), lambda b: (b, 0, 0)),
                  pl.BlockSpec((D_MODEL, 2 * XATTN_WIDTH), lambda b: (0, 0))],
        out_specs=[pl.BlockSpec((1, XATTN_WIDTH, XATTN_HEADS * N_MEM), lambda b: (b, 0, 0)),
                   pl.BlockSpec((1, XATTN_HEADS * N_MEM, 2 * XATTN_WIDTH), lambda b: (b, 0, 0))],
        out_shape=[jax.ShapeDtypeStruct((batch, XATTN_WIDTH, XATTN_HEADS * N_MEM), BF16),
                   jax.ShapeDtypeStruct((batch, XATTN_HEADS * N_MEM, 2 * XATTN_WIDTH), BF16)],
        name="memory_kv",
    )(mem, mem_kv_w.astype(BF16))


def _cross_attention(qx, kbd_ref, vo_ref):
    s = jnp.dot(qx.astype(BF16), kbd_ref[0], preferred_element_type=F32)
    probs = []
    for h in range(XATTN_HEADS):
        sh = s[:, h * N_MEM:(h + 1) * N_MEM]
        probs.append(jnp.exp(sh - jnp.max(sh, axis=-1, keepdims=True)).astype(BF16))
    nd = jnp.dot(jnp.concatenate(probs, axis=-1), vo_ref[0], preferred_element_type=F32)
    return nd[:, :XATTN_WIDTH] / nd[:, XATTN_WIDTH:]


def _layer_tail(x_ref, mix_ref, proj_ref, gate_start, wout_ref, lng_ref, lnb_ref, o_ref):
    for r0 in range(0, x_ref.shape[1], TAIL_ROWS):
        r = slice(r0, r0 + TAIL_ROWS)
        mixed = mix_ref[r, :] * _silu(proj_ref[r, gate_start:gate_start + MIX_WIDTH])
        y = jnp.dot(mixed.astype(BF16), wout_ref[...], preferred_element_type=F32)
        z = ALPHA * x_ref[0, r, :] + y
        zc = z - jnp.mean(z, axis=-1, keepdims=True)
        var = jnp.mean(zc * zc, axis=-1, keepdims=True)
        o_ref[0, r, :] = zc * lax.rsqrt(var + LN_EPS) * lng_ref[...] + lnb_ref[...]


class _Projector:
    def __init__(self, x_ref, win_ref, proj_ref, xb_ref):
        xb_ref[...] = x_ref[0].astype(BF16)
        self._refs = (win_ref, proj_ref, xb_ref)
        self._starts = list(range(0, proj_ref.shape[1], PROJ_PIECE))

    def _issue(self, c):
        win_ref, proj_ref, xb_ref = self._refs
        self._starts.remove(c)
        w = min(PROJ_PIECE, proj_ref.shape[1] - c)
        proj_ref[:, c:c + w] = jnp.dot(xb_ref[...], win_ref[:, c:c + w], preferred_element_type=F32)

    def step(self, count=1):
        for c in self._starts[:count]:
            self._issue(c)

    def need(self, start, width):
        for c in [c for c in self._starts if c < start + width and c + PROJ_PIECE > start]:
            self._issue(c)

    def through(self, column):
        self.need(0, column)

    def rest(self):
        self.step(len(self._starts))


class _Attender:
    def __init__(self, pieces, proj_ref, qx_start, kbd_ref, vo_ref, mix_ref):
        self._args = (pieces, proj_ref, qx_start, kbd_ref, vo_ref, mix_ref)
        self._blocks = list(range(0, mix_ref.shape[0], TAIL_ROWS))

    def step(self, count=1):
        pieces, proj_ref, qx_start, kbd_ref, vo_ref, mix_ref = self._args
        pieces.need(qx_start, XATTN_WIDTH)
        for r0 in self._blocks[:count]:
            self._blocks.remove(r0)
            r = slice(r0, r0 + TAIL_ROWS)
            mix_ref[r, TOK_WIDTH:] = _cross_attention(proj_ref[r, qx_start:qx_start + XATTN_WIDTH], kbd_ref, vo_ref)

    def rest(self):
        self.step(len(self._blocks))


def _columns(proj, start, width):
    return proj[:, start:start + width]


def _gmlp_layer_kernel(x_ref, win_ref, ws_ref, bs_ref, kbd_ref, vo_ref, wout_ref, lng_ref, lnb_ref, o_ref,
                       proj, xb_ref, tok_ref):
    pieces = _Projector(x_ref, win_ref, proj, xb_ref)
    pieces.through(TOK_WIDTH)
    u = jax.nn.gelu(_columns(proj, 0, TOK_WIDTH))
    pieces.through(2 * TOK_WIDTH)
    v = jax.nn.gelu(_columns(proj, TOK_WIDTH, TOK_WIDTH))
    rows = lax.broadcasted_iota(jnp.int32, (CHUNK, CHUNK), 0)
    cols = lax.broadcasted_iota(jnp.int32, (CHUNK, CHUNK), 1)
    causal = cols <= rows
    for g in range(N_HEADS):
        lanes = slice(g * HEAD_DIM, (g + 1) * HEAD_DIM)
        vg = v[:, lanes]
        vc = vg - jnp.mean(vg, axis=-1, keepdims=True)
        vn = (vc * lax.rsqrt(jnp.mean(vc * vc, axis=-1, keepdims=True) + LN_EPS)).astype(BF16)
        w = jnp.where(causal, ws_ref[g], 0.0).astype(BF16)
        bias = bs_ref[:, g:g + 1]
        pieces.step()
        for c in range(x_ref.shape[1] // CHUNK):
            r = slice(c * CHUNK, (c + 1) * CHUNK)
            spatial = jnp.dot(w, vn[r], preferred_element_type=F32) + bias
            tok_ref[r, lanes] = u[r, lanes] * spatial
    _Attender(pieces, proj, 2 * TOK_WIDTH, kbd_ref, vo_ref, tok_ref).rest()
    pieces.rest()
    _layer_tail(x_ref, tok_ref, proj, 2 * TOK_WIDTH + XATTN_WIDTH, wout_ref, lng_ref, lnb_ref, o_ref)


def _block_diag2(t):
    zero = jnp.zeros_like(t[:, :HEAD_DIM])
    return jnp.concatenate([jnp.concatenate([t[:, :HEAD_DIM], zero], axis=1),
                            jnp.concatenate([zero, t[:, HEAD_DIM:]], axis=1)], axis=0)


def _scale_blocks(pick, e_ref, idx, lanes):
    return jnp.concatenate(
        [pick(j)[j * SUB:(j + 1) * SUB, lanes] * e_ref[idx, j:j + 1, lanes] for j in range(N_SUB)],
        axis=0).astype(BF16)


def _hgrn_layer_kernel(layer, x_ref, win_ref, lbl_ref, ng_ref, kbd_ref, vo_ref, wout_ref, lng_ref, lnb_ref,
                       o_ref, proj, xb_ref, tok_ref, state_ref, gl_ref, s_ref, e_ref):
    @pl.when(pl.program_id(1) == 0)
    def _():
        state_ref[...] = jnp.zeros_like(state_ref)

    pieces = _Projector(x_ref, win_ref, proj, xb_ref)
    pieces.through(TOK_WIDTH)
    q = _silu(_columns(proj, 0, TOK_WIDTH))
    pieces.through(2 * TOK_WIDTH)
    f_logit = _columns(proj, TOK_WIDTH, TOK_WIDTH)

    lbl = lbl_ref[...]
    lbe = jnp.exp(lbl - jnp.max(lbl, axis=0, keepdims=True))
    lbp = lbe / jnp.sum(lbe, axis=0, keepdims=True)
    lb = jnp.sum(lbp[:layer + 1], axis=0, keepdims=True) - lbp[0:1]
    f = lb + (1.0 - lb) * jax.nn.sigmoid(f_logit)
    pieces.through(3 * TOK_WIDTH)
    log_f = jnp.log(f)
    k = 1.0 - f
    val = _columns(proj, 2 * TOK_WIDTH, TOK_WIDTH)

    tt = lax.broadcasted_iota(jnp.int32, (CHUNK, PAIR), 0)
    ss = lax.broadcasted_iota(jnp.int32, (CHUNK, PAIR), 1) % CHUNK
    level_masks = [((tt // (2 * h)) == (ss // (2 * h))) & ((tt % (2 * h)) >= h) & ((ss % (2 * h)) < h)
                   for h in SPLIT_HALVES]
    diag_mask = ((tt // SUB) == (ss // SUB)) & (ss <= tt)
    br = lax.broadcasted_iota(jnp.int32, (CHUNK, CHUNK), 0)
    bc = lax.broadcasted_iota(jnp.int32, (CHUNK, CHUNK), 1)
    block_tril = jnp.where(((br // SUB) == (bc // SUB)) & (bc <= br), 1.0, 0.0).astype(BF16)
    sub_row = lax.broadcasted_iota(jnp.int32, (N_SUB, 1), 0)
    n_lvl = len(SPLIT_HALVES)

    n_chunks = x_ref.shape[1] // CHUNK
    ql_all, kl_all, decay_all = [], [], []
    for c in range(n_chunks):
        r = slice(c * CHUNK, (c + 1) * CHUNK)
        lfc = log_f[r]
        hi = lfc.astype(BF16)
        lo = (lfc - hi.astype(F32)).astype(BF16)
        gl = (jnp.dot(block_tril, hi, preferred_element_type=F32)
              + jnp.dot(block_tril, lo, preferred_element_type=F32))
        ql_all.append(q[r] * jnp.exp(gl))
        kl_all.append(k[r] * jnp.exp(-gl))

        for h in range(N_HEADS):
            gl_ref[c, h] = gl[:, h * HEAD_DIM:(h + 1) * HEAD_DIM]
        bsum = jnp.concatenate(
            [gl_ref[c, h, pl.ds(SUB - 1, N_SUB, stride=SUB), :] for h in range(N_HEADS)], axis=1)
        incl = bsum
        for d in (1, 2, 4):
            incl = incl + jnp.where(sub_row >= d, pltpu.roll(incl, d, 0), 0.0)
        s = incl - bsum
        s_ref[c] = s
        total = s_ref[c, N_SUB - 1:N_SUB, :] + gl[CHUNK - 1:CHUNK, :]
        for li, half in enumerate(SPLIT_HALVES):
            nb = half // SUB
            boundary = s_ref[c, nb:nb + 1, :]
            for g0 in range(2 * nb, N_SUB, 2 * nb):
                boundary = jnp.where(sub_row >= g0, s_ref[c, g0 + nb:g0 + nb + 1, :], boundary)
            e_ref[c, li] = jnp.exp(-jnp.abs(s - boundary))
        e_ref[c, n_lvl] = jnp.exp(s)
        e_ref[c, n_lvl + 1] = jnp.exp(total - s)
        decay_all.append(jnp.exp(total))
        pieces.step()

    pair_lanes = [slice(p * PAIR, (p + 1) * PAIR) for p in range(N_HEADS // 2)]
    zero = jnp.zeros((HEAD_DIM, HEAD_DIM), BF16)
    attend = _Attender(pieces, proj, 3 * TOK_WIDTH, kbd_ref, vo_ref, tok_ref)
    for c in range(n_chunks):
        r = slice(c * CHUNK, (c + 1) * CHUNK)
        ql, kl, e_c = ql_all[c], kl_all[c], e_ref.at[c]
        scores = [jnp.where(diag_mask, _dot_nt(ql[:, lanes], _block_diag2(kl[:, lanes].astype(BF16))), 0.0)
                  for lanes in pair_lanes]
        for li, half in enumerate(SPLIT_HALVES):
            nb = half // SUB
            for p, lanes in enumerate(pair_lanes):
                xs = _scale_blocks(lambda j: ql if (j // nb) % 2 else kl, e_c, li, lanes)
                scores[p] = jnp.where(level_masks[li], _dot_nt(xs, _block_diag2(xs)), scores[p])
        outs, k_ends, vcs = [], [], []
        for p, lanes in enumerate(pair_lanes):
            q_dec = _scale_blocks(lambda j: ql, e_c, n_lvl, lanes)
            k_ends.append(_scale_blocks(lambda j: kl, e_c, n_lvl + 1, lanes))
            vcs.append(val[r, lanes].astype(BF16))
            state_bd = jnp.concatenate(
                [jnp.concatenate([state_ref[2 * p].astype(BF16), zero], axis=1),
                 jnp.concatenate([zero, state_ref[2 * p + 1].astype(BF16)], axis=1)], axis=0)
            outs.append(_dot(scores[p], _block_diag2(vcs[p])) + _dot_nt(q_dec, state_bd))
        for p in range(N_HEADS // 2):
            for i in range(2):
                h = 2 * p + i
                hl = slice(i * HEAD_DIM, (i + 1) * HEAD_DIM)
                head_lanes = slice(h * HEAD_DIM, (h + 1) * HEAD_DIM)
                state_ref[h] = (state_ref[h] * decay_all[c][:, head_lanes]
                                + _dot_tn(vcs[p][:, hl], k_ends[p][:, hl]))
                oh = outs[p][:, hl]
                oh = oh * lax.rsqrt(jnp.mean(oh * oh, axis=-1, keepdims=True) + RMS_EPS)
                tok_ref[r, head_lanes] = oh * ng_ref[:, head_lanes]
        pieces.step()
        if c % 2 == 0:
            attend.step()
    attend.rest()
    pieces.rest()
    _layer_tail(x_ref, tok_ref, proj, 3 * TOK_WIDTH + XATTN_WIDTH, wout_ref, lng_ref, lnb_ref, o_ref)


def _pool_layer_kernel(x_ref, win_ref, wpool_ref, scale_ref, kbd_ref, vo_ref, wout_ref, lng_ref, lnb_ref, o_ref,
                       proj, xb_ref, tok_ref, pbuf_ref):
    si = pl.program_id(1)
    tile = x_ref.shape[1]

    @pl.when(si == 0)
    def _():
        pbuf_ref[:POOL_HALO] = jnp.zeros((POOL_HALO, TOK_WIDTH), F32)

    pieces = _Projector(x_ref, win_ref, proj, xb_ref)
    pieces.through(TOK_WIDTH)
    p = _columns(proj, 0, TOK_WIDTH)

    pbuf_ref[POOL_HALO:] = p
    ext = pbuf_ref[...]
    pieces.step(2)
    sums = []
    acc = ext
    for w in POOL_WINDOWS:
        acc = acc + pltpu.roll(acc, w // 2, 0)
        sums.append(acc[POOL_HALO:])
    pbuf_ref[:POOL_HALO] = p[tile - POOL_HALO:]

    pos = si * tile + lax.broadcasted_iota(jnp.int32, (tile, 1), 0)
    chan = lax.broadcasted_iota(jnp.int32, (1, TOK_WIDTH), 1)
    pooled = None
    for gi in reversed(range(len(POOL_WINDOWS))):
        w = POOL_WINDOWS[gi]
        mean = sums[gi] * (1.0 / jnp.minimum(pos + 1, w).astype(F32))
        pooled = mean if pooled is None else jnp.where(chan < (gi + 1) * POOL_GROUP, mean, pooled)
    pieces.rest()
    tok_ref[:, :TOK_WIDTH] = _dot(pooled - p, wpool_ref[...]) * scale_ref[...]
    _Attender(pieces, proj, TOK_WIDTH, kbd_ref, vo_ref, tok_ref).rest()
    _layer_tail(x_ref, tok_ref, proj, TOK_WIDTH + XATTN_WIDTH, wout_ref, lng_ref, lnb_ref, o_ref)


def _lru_layer_kernel(x_ref, win_ref, convw_ref, convb_ref, wg_ref, bg_ref, ap_ref, kbd_ref, vo_ref, wout_ref,
                      lng_ref, lnb_ref, o_ref, proj, xb_ref, tok_ref, halo_ref, seg_in_ref, seg_out_ref, carry_ref,
                      segp_ref, segh_ref):
    si = pl.program_id(1)
    tile = x_ref.shape[1]

    @pl.when(si == 0)
    def _():
        halo_ref[...] = jnp.zeros_like(halo_ref)
        carry_ref[...] = jnp.zeros_like(carry_ref)

    pieces = _Projector(x_ref, win_ref, proj, xb_ref)
    seg = tile // SEGMENTS
    seg_stride = seg + SEGMENT_PAD
    rows = [pl.ds(r, SEGMENTS, stride=seg_stride) for r in range(seg)]
    ap = ap_ref[...]
    neg_softplus = -(jnp.maximum(-ap, 0.0) + jnp.log1p(jnp.exp(-jnp.abs(ap))))
    first = (si * tile + lax.broadcasted_iota(jnp.int32, (tile, 1), 0)) == 0
    sublane = lax.broadcasted_iota(jnp.int32, (SEGMENTS, 1), 0)
    for h in range(N_HEADS):
        lanes = slice(h * HEAD_DIM, (h + 1) * HEAD_DIM)
        pieces.need(h * HEAD_DIM, HEAD_DIM)
        xin = _columns(proj, h * HEAD_DIM, HEAD_DIM)
        for s in range(SEGMENTS):
            seg_in_ref[h, s * seg_stride:s * seg_stride + seg] = xin[s * seg:(s + 1) * seg]
        slabs = [seg_in_ref[h, r, :] for r in rows]
        history = [jnp.where(sublane == 0, halo_ref[CONV_HALO - k:CONV_HALO - k + 1, lanes],
                             pltpu.roll(slabs[seg - k], 1, 0)) for k in range(1, CONV_WIDTH)]
        halo_ref[:, lanes] = xin[tile - CONV_HALO:]
        taps = [convw_ref[CONV_WIDTH - 1 - j:CONV_WIDTH - j, lanes] for j in range(CONV_WIDTH)]
        conv = []
        for r in range(seg):
            acc = taps[0] * slabs[r] + convb_ref[:, lanes]
            for j in range(1, CONV_WIDTH):
                acc = acc + taps[j] * (slabs[r - j] if r >= j else history[j - r - 1])
            conv.append(acc)
        xh = jnp.concatenate(conv, axis=0)
        pieces.step()
        gates = jax.nn.sigmoid(_dot(xh, wg_ref[h]) + bg_ref[h])
        log_a = LRU_C * gates[:, HEAD_DIM:] * neg_softplus[:, lanes]
        th = jnp.tanh(-log_a)
        sq = 2.0 * th / (1.0 + th)
        mult = jnp.where(first, 1.0, jnp.where(sq > 0.0, sq * lax.rsqrt(sq), 0.0))
        a_val = jnp.exp(log_a)
        b_val = mult * gates[:, :HEAD_DIM] * xh

        slab_rows = [slice(SEGMENTS * r, SEGMENTS * (r + 1)) for r in range(seg)]
        acc = jnp.zeros((SEGMENTS, HEAD_DIM), F32)
        prod = jnp.ones((SEGMENTS, HEAD_DIM), F32)
        for r in slab_rows:
            acc = a_val[r] * acc + b_val[r]
            prod = prod * a_val[r]
        segp_ref[h] = prod
        segh_ref[h] = acc
        for s in range(1, SEGMENTS):
            carry_ref[h, s:s + 1] = (segp_ref[h, s - 1:s] * carry_ref[h, s - 1:s] + segh_ref[h, s - 1:s])
        acc = carry_ref[h]
        for r, strided in zip(slab_rows, rows):
            acc = a_val[r] * acc + b_val[r]
            seg_out_ref[h, strided, :] = acc
        last = (SEGMENTS - 1) * seg_stride + seg - 1
        carry_ref[h, 0:1] = seg_out_ref[h, last:last + 1]
        for s in range(SEGMENTS):
            tok_ref[s * seg:(s + 1) * seg, lanes] = seg_out_ref[h, s * seg_stride:s * seg_stride + seg]
    pieces.rest()
    _Attender(pieces, proj, TOK_WIDTH, kbd_ref, vo_ref, tok_ref).rest()
    _layer_tail(x_ref, tok_ref, proj, TOK_WIDTH + XATTN_WIDTH, wout_ref, lng_ref, lnb_ref, o_ref)


def _full(arr):
    nd = arr.ndim
    return pl.BlockSpec(arr.shape, lambda b, s: (0,) * nd, pipeline_mode=pl.Buffered(1))


def _with_rounded_weights(body, n_params, x_ref, win_ref, *refs):
    others = refs[:n_params]
    kbd_ref, vo_ref, wout_ref, lng_ref, lnb_ref, o_ref, win_s, wout_s = refs[n_params:n_params + 8]
    scratch = refs[n_params + 8:]

    @pl.when((pl.program_id(0) == 0) & (pl.program_id(1) == 0))
    def _():
        for c in range(0, win_s.shape[1], PROJ_PIECE):
            win_s[:, c:c + PROJ_PIECE] = win_ref[0, :, c:c + PROJ_PIECE].astype(BF16)
        for c in range(0, wout_s.shape[1], PROJ_PIECE):
            wout_s[:, c:c + PROJ_PIECE] = wout_ref[0, :, c:c + PROJ_PIECE].astype(BF16)

    body(x_ref, win_s, *others, kbd_ref, vo_ref, wout_s, lng_ref, lnb_ref, o_ref, *scratch)


def _run_layer(body, x, w_in, j, params, kbd, vo, w_out, i, ln_g, ln_b, scratch, name):
    batch, seq, _ = x.shape
    tile = SEQ_TILE
    in_width = w_in.shape[2]
    x_spec = pl.BlockSpec((1, tile, D_MODEL), lambda b, s: (b, s, 0))
    norm = [ln_g[i].reshape(1, D_MODEL), ln_b[i].reshape(1, D_MODEL)]
    in_specs = ([x_spec,
                 pl.BlockSpec((1, D_MODEL, in_width), lambda b, s: (j, 0, 0), pipeline_mode=pl.Buffered(1))]
                + [_full(p) for p in params]
                + [pl.BlockSpec((1,) + kbd.shape[1:], lambda b, s: (b, 0, 0)),
                   pl.BlockSpec((1,) + vo.shape[1:], lambda b, s: (b, 0, 0)),
                   pl.BlockSpec((1, MIX_WIDTH, D_MODEL), lambda b, s: (i, 0, 0), pipeline_mode=pl.Buffered(1))]
                + [_full(p) for p in norm])
    common = [pltpu.VMEM((D_MODEL, in_width), BF16),
              pltpu.VMEM((MIX_WIDTH, D_MODEL), BF16),
              pltpu.VMEM((tile, in_width), F32),
              pltpu.VMEM((tile, D_MODEL), BF16),
              pltpu.VMEM((tile, MIX_WIDTH), F32)]
    return pl.pallas_call(
        functools.partial(_with_rounded_weights, body, len(params)),
        grid=(batch, seq // tile),
        in_specs=in_specs,
        out_specs=x_spec,
        out_shape=jax.ShapeDtypeStruct(x.shape, x.dtype),
        scratch_shapes=common + scratch,
        compiler_params=pltpu.CompilerParams(
            dimension_semantics=("arbitrary", "arbitrary"), vmem_limit_bytes=VMEM_LIMIT_BYTES),
        name=name,
    )(x, w_in, *params, kbd, vo, w_out, *norm)


def kernel(x, mem, mem_kv_w, ln_g, ln_b, w_out, hgrn_lb_logits, a_w_in, a_w_s, a_b_s, b_w_in, b_norm_g,
           c_w_in, c_w_pool, c_scale, d_w_in, d_conv_w, d_conv_b, d_w_gx, d_b_gx, d_w_ga, d_b_ga,
           d_a_param):
    tile = SEQ_TILE
    assert x.shape[1] % tile == 0 and tile % CHUNK == 0
    kbd, vo = _memory_operands(mem, mem_kv_w)
    for i in range(DEPTH):
        kind, j = i % 4, i // 4
        if kind == 0:
            w_in, params = a_w_in, [a_w_s[j], a_b_s[j].T]
            body, scratch = _gmlp_layer_kernel, []
        elif kind == 1:
            w_in, params = b_w_in, [hgrn_lb_logits, b_norm_g[j].reshape(1, TOK_WIDTH)]
            body = functools.partial(_hgrn_layer_kernel, i)
            scratch = [pltpu.VMEM((N_HEADS, HEAD_DIM, HEAD_DIM), F32),
                       pltpu.VMEM((tile // CHUNK, N_HEADS, CHUNK, HEAD_DIM), F32),
                       pltpu.VMEM((tile // CHUNK, N_SUB, TOK_WIDTH), F32),
                       pltpu.VMEM((tile // CHUNK, len(SPLIT_HALVES) + 2, N_SUB, TOK_WIDTH), F32)]
        elif kind == 2:
            wpool = jax.scipy.linalg.block_diag(*[c_w_pool[j, g] for g in range(len(POOL_WINDOWS))])
            w_in, params = c_w_in, [wpool.astype(BF16), c_scale[j].reshape(1, TOK_WIDTH)]
            body = _pool_layer_kernel
            scratch = [pltpu.VMEM((POOL_HALO + tile, TOK_WIDTH), F32)]
        else:
            wg = jnp.concatenate([d_w_gx[j], d_w_ga[j]], axis=-1).astype(BF16)
            bg = jnp.concatenate([d_b_gx[j], d_b_ga[j]], axis=-1)[:, None, :]
            w_in = d_w_in
            params = [d_conv_w[j], d_conv_b[j].reshape(1, TOK_WIDTH), wg, bg, d_a_param[j].reshape(1, TOK_WIDTH)]
            body = _lru_layer_kernel
            scratch = [pltpu.VMEM((CONV_HALO, TOK_WIDTH), F32),
                       pltpu.VMEM((N_HEADS, tile + SEGMENTS * SEGMENT_PAD, HEAD_DIM), F32),
                       pltpu.VMEM((N_HEADS, tile + SEGMENTS * SEGMENT_PAD, HEAD_DIM), F32),
                       pltpu.VMEM((N_HEADS, SEGMENTS, HEAD_DIM), F32),
                       pltpu.VMEM((N_HEADS, SEGMENTS, HEAD_DIM), F32),
                       pltpu.VMEM((N_HEADS, SEGMENTS, HEAD_DIM), F32)]
        x = _run_layer(body, x, w_in, j, params, kbd, vo, w_out, i, ln_g, ln_b, scratch, name=f"layer{i}")
    return x
```
